```python
import jax, jax.numpy as jnp
from jax import lax
import numpy as np

D_MODEL = 4096
BATCH = 2
SEQ = 4096
DEPTH = 2

MIX_WIDTH = D_MODEL
CONV_WIDTH = D_MODEL // 2
ATTN_WIDTH = MIX_WIDTH - CONV_WIDTH
HEAD_DIM = 128
N_HEADS = ATTN_WIDTH // HEAD_DIM
N_KV_HEADS = 4
KV_WIDTH = N_KV_HEADS * HEAD_DIM
GROUP = N_HEADS // N_KV_HEADS
IDX_HEADS = 32
IDX_DIM = 64
TOPK_MAX = 256
CONV_KERNEL = 31
D_FF = 2 * D_MODEL
FFN_RES = 0.5
Q_BLOCK = 128
N_MOD = 9
EPS = 1e-6
COL_SIZES = (CONV_WIDTH, CONV_WIDTH, ATTN_WIDTH, KV_WIDTH, KV_WIDTH,
             IDX_HEADS * IDX_DIM, IDX_DIM, IDX_HEADS)
N_IN = int(sum(COL_SIZES))

kernel_name = "hymba_conformer_dsa_macaron_adaln"


def _rmsnorm(x, g):
    xf = x.astype(jnp.float32)
    y = xf * lax.rsqrt(jnp.mean(xf * xf, axis=-1, keepdims=True) + EPS)
    return (y * g.astype(jnp.float32)).astype(x.dtype)


def _layernorm(x, g, b):
    xf = x.astype(jnp.float32)
    mu = jnp.mean(xf, axis=-1, keepdims=True)
    var = jnp.mean(jnp.square(xf - mu), axis=-1, keepdims=True)
    y = (xf - mu) * lax.rsqrt(var + EPS)
    return (y * g.astype(jnp.float32) + b.astype(jnp.float32)).astype(x.dtype)


def _modulate(h, shift, scale):
    return h * (1.0 + scale[:, None, :]) + shift[:, None, :]


def _swiglu(h, w_gu, w_d):
    g, u = jnp.split(h @ w_gu, 2, axis=-1)
    return (jax.nn.silu(g) * u) @ w_d


def _conformer_conv(a, gate, conv_w, conv_b, ln_g, ln_b):
    u = a * jax.nn.sigmoid(gate)
    y = lax.conv_general_dilated(
        u, conv_w[:, None, :].astype(u.dtype), window_strides=(1,),
        padding=[(CONV_KERNEL - 1, 0)],
        dimension_numbers=("NWC", "WIO", "NWC"),
        feature_group_count=CONV_WIDTH) + conv_b
    return jax.nn.silu(_layernorm(y, ln_g, ln_b))


def _dsa_attention(q, k, v, q_idx, k_idx, w_idx):
    B, T = q.shape[0], q.shape[1]
    topk = min(TOPK_MAX, T // 4)
    nblk = T // Q_BLOCK
    idx_scale = (IDX_DIM ** -0.5) * (IDX_HEADS ** -0.5)
    attn_scale = HEAD_DIM ** -0.5
    key_pos = jnp.arange(T, dtype=jnp.int32)
    k_idx_f = k_idx.astype(jnp.float32)

    def to_blocks(t):
        return jnp.swapaxes(t.reshape((B, nblk, Q_BLOCK) + t.shape[2:]), 0, 1)

    xs = (to_blocks(q), to_blocks(q_idx), to_blocks(w_idx),
          key_pos.reshape(nblk, Q_BLOCK))

    def block(args):
        qb, qib, wb, posb = args
        s_h = jnp.einsum("bqhd,bsd->bqhs", qib.astype(jnp.float32), k_idx_f)
        score = jnp.einsum("bqhs,bqh->bqs", jax.nn.relu(s_h),
                           wb.astype(jnp.float32)) * idx_scale
        causal = key_pos[None, None, :] <= posb[None, :, None]
        score = jnp.where(causal, score, -jnp.inf)
        _, sel = lax.top_k(score, topk)
        valid = sel <= posb[None, :, None]
        k_sel = jax.vmap(lambda kk, ii: kk[ii])(k, sel)
        v_sel = jax.vmap(lambda vv, ii: vv[ii])(v, sel)
        qg = qb.reshape(B, Q_BLOCK, N_KV_HEADS, GROUP, HEAD_DIM)
        logits = jnp.einsum("bqhgd,bqkhd->bqhgk", qg.astype(jnp.float32),
                            k_sel.astype(jnp.float32)) * attn_scale
        logits = jnp.where(valid[:, :, None, None, :], logits, -jnp.inf)
        p = jax.nn.softmax(logits, axis=-1)
        o = jnp.einsum("bqhgk,bqkhd->bqhgd", p.astype(v.dtype), v_sel)
        return o.reshape(B, Q_BLOCK, ATTN_WIDTH)

    out = lax.map(block, xs)
    return jnp.swapaxes(out, 0, 1).reshape(B, T, ATTN_WIDTH)


def _token_mix(h, w_in, conv_w, conv_b, conv_ln_g, conv_ln_b,
               conv_out_norm, attn_out_norm, w_out):
    B, T, _ = h.shape
    offsets = np.cumsum(COL_SIZES)[:-1].tolist()
    (a, gate, q, k, v, qi, ki, wi) = jnp.split(h @ w_in, offsets, axis=-1)
    y_conv = _conformer_conv(a, gate, conv_w, conv_b, conv_ln_g, conv_ln_b)
    y_attn = _dsa_attention(
        q.reshape(B, T, N_HEADS, HEAD_DIM),
        k.reshape(B, T, N_KV_HEADS, HEAD_DIM),
        v.reshape(B, T, N_KV_HEADS, HEAD_DIM),
        qi.reshape(B, T, IDX_HEADS, IDX_DIM), ki, wi)
    y = jnp.concatenate([_rmsnorm(y_conv, conv_out_norm),
                         _rmsnorm(y_attn, attn_out_norm)], axis=-1)
    return y @ w_out


def setup_inputs(seed: int = 0) -> dict:
    key = jax.random.key(seed)
    ks = jax.random.split(key, 24)
    f32 = jnp.float32
    L = DEPTH

    def nrm(k, shape, scale):
        return jax.random.normal(k, shape, f32) * scale

    def gain(k, shape):
        return 1.0 + 0.01 * jax.random.normal(k, shape, f32)

    return {
        "x": nrm(ks[0], (BATCH, SEQ, D_MODEL), 1.0),
        "c": nrm(ks[1], (BATCH, D_MODEL), 1.0),
        "ada_w": nrm(ks[2], (L, D_MODEL, N_MOD * D_MODEL), 0.5 * D_MODEL ** -0.5),
        "ada_b": nrm(ks[3], (L, N_MOD * D_MODEL), 0.01),
        "ffn1_norm": gain(ks[4], (L, D_MODEL)),
        "ffn1_wgu": nrm(ks[5], (L, D_MODEL, 2 * D_FF), D_MODEL ** -0.5),
        "ffn1_wd": nrm(ks[6], (L, D_FF, D_MODEL), D_FF ** -0.5),
        "mix_norm": gain(ks[7], (L, D_MODEL)),
        "w_in": nrm(ks[8], (L, D_MODEL, N_IN), D_MODEL ** -0.5),
        "conv_w": nrm(ks[9], (L, CONV_KERNEL, CONV_WIDTH), CONV_KERNEL ** -0.5),
        "conv_b": nrm(ks[10], (L, CONV_WIDTH), 0.01),
        "conv_ln_g": gain(ks[11], (L, CONV_WIDTH)),
        "conv_ln_b": nrm(ks[12], (L, CONV_WIDTH), 0.01),
        "conv_out_norm": gain(ks[13], (L, CONV_WIDTH)),
        "attn_out_norm": gain(ks[14], (L, ATTN_WIDTH)),
        "w_out": nrm(ks[15], (L, MIX_WIDTH, D_MODEL), MIX_WIDTH ** -0.5),
        "ffn2_norm": gain(ks[16], (L, D_MODEL)),
        "ffn2_wgu": nrm(ks[17], (L, D_MODEL, 2 * D_FF), D_MODEL ** -0.5),
        "ffn2_wd": nrm(ks[18], (L, D_FF, D_MODEL), D_FF ** -0.5),
        "final_norm": gain(ks[19], (D_MODEL,)),
    }


def reference(x, c, ada_w, ada_b, ffn1_norm, ffn1_wgu, ffn1_wd, mix_norm, w_in,
              conv_w, conv_b, conv_ln_g, conv_ln_b, conv_out_norm, attn_out_norm,
              w_out, ffn2_norm, ffn2_wgu, ffn2_wd, final_norm):
    c_act = jax.nn.silu(c)
    for l in range(DEPTH):
        mod = c_act @ ada_w[l] + ada_b[l]
        (sh1, sc1, g1, sh2, sc2, g2, sh3, sc3, g3) = jnp.split(mod, N_MOD, axis=-1)
        h = _modulate(_rmsnorm(x, ffn1_norm[l]), sh1, sc1)
        x = x + FFN_RES * g1[:, None, :] * _swiglu(h, ffn1_wgu[l], ffn1_wd[l])
        h = _modulate(_rmsnorm(x, mix_norm[l]), sh2, sc2)
        x = x + g2[:, None, :] * _token_mix(h, w_in[l], conv_w[l], conv_b[l],
                                            conv_ln_g[l], conv_ln_b[l],
                                            conv_out_norm[l], attn_out_norm[l],
                                            w_out[l])
        h = _modulate(_rmsnorm(x, ffn2_norm[l]), sh3, sc3)
        x = x + FFN_RES * g3[:, None, :] * _swiglu(h, ffn2_wgu[l], ffn2_wd[l])
    return _rmsnorm(x, final_norm)
```

```python
import functools
import math

import jax
import jax.numpy as jnp
from jax import lax
from jax.experimental import pallas as pl
from jax.experimental.pallas import tpu as pltpu

HEAD_DIM = 128
N_KV_HEADS = 4
IDX_HEADS = 32
IDX_DIM = 64
TOPK_MAX = 256
CONV_KERNEL = 31
Q_BLOCK = 128
N_MOD = 9
EPS = 1e-6
FFN_RES = 0.5

LANE = 128
SUBLANE = 8
KEY_CHUNK = 256
ATT_CHUNK = 512
CONV_HALO = 32
VMEM_LIMIT = 56 * 1024 * 1024

INT_MIN = -(2 ** 31)
NEG_BIAS = -2e30
M_INIT = -1e30

F32 = jnp.float32
BF16 = jnp.bfloat16


def _cparams(n_axes):
    return pltpu.CompilerParams(dimension_semantics=("parallel",) * n_axes, vmem_limit_bytes=VMEM_LIMIT)


def _mod_kernel(cb_ref, w_ref, b_ref, o_ref, *, k_chunk):
    nb, kdim, _ = cb_ref.shape
    tn = w_ref.shape[2]

    def body(kc, accs):
        k0 = pl.multiple_of(kc * k_chunk, k_chunk)
        w = w_ref[0, pl.ds(k0, k_chunk), :]
        out = []
        for b in range(nb):
            cb = cb_ref[b, pl.ds(k0, k_chunk), :]
            cb = cb * jax.nn.sigmoid(cb)
            p = w * jnp.tile(cb, (1, tn // LANE))
            out.append(accs[b] + p.reshape(k_chunk // SUBLANE, SUBLANE, tn).sum(axis=0))
        return tuple(out)

    accs = lax.fori_loop(0, kdim // k_chunk, body,
                         tuple(jnp.zeros((SUBLANE, tn), F32) for _ in range(nb)))
    for b in range(nb):
        o_ref[0, b:b + 1, :] = accs[b].sum(axis=0, keepdims=True) + b_ref[0]


def _modulation(c, ada_w, ada_b, *, tn=512, k_chunk=128):
    nl, kdim, n = ada_w.shape
    nb = c.shape[0]
    cb = jnp.broadcast_to(c[:, :, None], (nb, kdim, LANE))
    return pl.pallas_call(
        functools.partial(_mod_kernel, k_chunk=k_chunk),
        out_shape=jax.ShapeDtypeStruct((nl, nb, n), F32),
        grid=(nl, n // tn),
        in_specs=[
            pl.BlockSpec((nb, kdim, LANE), lambda l, j: (0, 0, 0)),
            pl.BlockSpec((1, kdim, tn), lambda l, j: (l, 0, j)),
            pl.BlockSpec((1, 1, tn), lambda l, j: (l, 0, j)),
        ],
        out_specs=pl.BlockSpec((1, nb, tn), lambda l, j: (l, 0, j)),
        compiler_params=_cparams(2),
        name="adaln_mod",
    )(cb, ada_w, ada_b.reshape(nl, 1, n))


def _norm_mod_kernel(x_ref, g_ref, sc_ref, sh_ref, o_ref):
    x = x_ref[...]
    y = x * lax.rsqrt(jnp.mean(x * x, axis=-1, keepdims=True) + EPS) * g_ref[...]
    o_ref[...] = (y * (1.0 + sc_ref[0]) + sh_ref[0]).astype(o_ref.dtype)


def _norm_mod(x2d, gain, scale, shift, seq, *, tm=512):
    m, d = x2d.shape
    nb = scale.shape[0]
    per_b = seq // tm
    return pl.pallas_call(
        _norm_mod_kernel,
        out_shape=jax.ShapeDtypeStruct((m, d), BF16),
        grid=(m // tm,),
        in_specs=[
            pl.BlockSpec((tm, d), lambda i: (i, 0)),
            pl.BlockSpec((1, d), lambda i: (0, 0)),
            pl.BlockSpec((1, 1, d), lambda i: (i // per_b, 0, 0)),
            pl.BlockSpec((1, 1, d), lambda i: (i // per_b, 0, 0)),
        ],
        out_specs=pl.BlockSpec((tm, d), lambda i: (i, 0)),
        compiler_params=_cparams(1),
        name="norm_mod",
    )(x2d, gain.reshape(1, d), scale.reshape(nb, 1, d), shift.reshape(nb, 1, d))


def _rmsnorm_kernel(x_ref, g_ref, o_ref):
    x = x_ref[...]
    o_ref[...] = x * lax.rsqrt(jnp.mean(x * x, axis=-1, keepdims=True) + EPS) * g_ref[...]


def _final_norm(x2d, gain, *, tm=512):
    m, d = x2d.shape
    return pl.pallas_call(
        _rmsnorm_kernel,
        out_shape=jax.ShapeDtypeStruct((m, d), F32),
        grid=(m // tm,),
        in_specs=[pl.BlockSpec((tm, d), lambda i: (i, 0)), pl.BlockSpec((1, d), lambda i: (0, 0))],
        out_specs=pl.BlockSpec((tm, d), lambda i: (i, 0)),
        compiler_params=_cparams(1),
        name="final_norm",
    )(x2d, gain.reshape(1, d))


def _mm_kernel(a_ref, w_ref, o_ref):
    o_ref[...] = jnp.dot(a_ref[...], w_ref[...], preferred_element_type=F32).astype(o_ref.dtype)


def _matmul(a, w, *, col0, ncols, out_dtype, bm, bn):
    m, k = a.shape
    off = col0 // bn
    return pl.pallas_call(
        _mm_kernel,
        out_shape=jax.ShapeDtypeStruct((m, ncols), out_dtype),
        grid=(m // bm, ncols // bn),
        in_specs=[pl.BlockSpec((bm, k), lambda i, j: (i, 0)),
                  pl.BlockSpec((k, bn), lambda i, j: (0, j + off))],
        out_specs=pl.BlockSpec((bm, bn), lambda i, j: (i, j)),
        compiler_params=_cparams(2),
        name="matmul",
    )(a, w)


def _mm_swiglu_kernel(a_ref, wg_ref, wu_ref, o_ref):
    a = a_ref[...]
    g = jnp.dot(a, wg_ref[...], preferred_element_type=F32)
    u = jnp.dot(a, wu_ref[...], preferred_element_type=F32)
    o_ref[...] = (g * jax.nn.sigmoid(g) * u).astype(o_ref.dtype)


def _matmul_swiglu(a, wgu, *, bm=1024, bn=512):
    m, k = a.shape
    f = wgu.shape[1] // 2
    nj = f // bn
    return pl.pallas_call(
        _mm_swiglu_kernel,
        out_shape=jax.ShapeDtypeStruct((m, f), BF16),
        grid=(m // bm, nj),
        in_specs=[pl.BlockSpec((bm, k), lambda i, j: (i, 0)),
                  pl.BlockSpec((k, bn), lambda i, j: (0, j)),
                  pl.BlockSpec((k, bn), lambda i, j: (0, j + nj))],
        out_specs=pl.BlockSpec((bm, bn), lambda i, j: (i, j)),
        compiler_params=_cparams(2),
        name="matmul_swiglu",
    )(a, wgu, wgu)


def _mm_res_kernel(a_ref, w_ref, x_ref, gate_ref, o_ref, *, coef):
    y = jnp.dot(a_ref[...], w_ref[...], preferred_element_type=F32)
    o_ref[...] = x_ref[...] + (coef * gate_ref[0]) * y


def _matmul_residual(a, w, x2d, gate, seq, *, coef, bm, bn):
    m, k = a.shape
    n = w.shape[1]
    nb = gate.shape[0]
    per_b = seq // bm
    return pl.pallas_call(
        functools.partial(_mm_res_kernel, coef=coef),
        out_shape=jax.ShapeDtypeStruct((m, n), F32),
        grid=(m // bm, n // bn),
        in_specs=[pl.BlockSpec((bm, k), lambda i, j: (i, 0)),
                  pl.BlockSpec((k, bn), lambda i, j: (0, j)),
                  pl.BlockSpec((bm, bn), lambda i, j: (i, j)),
                  pl.BlockSpec((1, 1, bn), lambda i, j: (i // per_b, 0, j))],
        out_specs=pl.BlockSpec((bm, bn), lambda i, j: (i, j)),
        compiler_params=_cparams(2),
        name="matmul_residual",
    )(a, w, x2d, gate.reshape(nb, 1, n))


def _conv_kernel(a_ref, g_ref, ap_ref, gp_ref, cw_ref, cb_ref, lng_ref, lnb_ref, on_ref, o_ref,
                 u_scr, y_scr, *, lane_chunk):
    i = pl.program_id(1)
    tt, c = a_ref.shape[1], a_ref.shape[2]
    u_prev = ap_ref[0] * jax.nn.sigmoid(gp_ref[0])
    u_scr[0:CONV_HALO, :] = jnp.where(i > 0, u_prev, 0.0)
    u_scr[CONV_HALO:CONV_HALO + tt, :] = a_ref[0] * jax.nn.sigmoid(g_ref[0])
    base = CONV_HALO - (CONV_KERNEL - 1)
    for c0 in range(0, c, lane_chunk):
        acc = jnp.broadcast_to(cb_ref[:, c0:c0 + lane_chunk], (tt, lane_chunk))
        for j in range(CONV_KERNEL):
            acc = acc + cw_ref[j:j + 1, c0:c0 + lane_chunk] * u_scr[base + j:base + j + tt, c0:c0 + lane_chunk]
        y_scr[:, c0:c0 + lane_chunk] = acc
    y = y_scr[...]
    mu = jnp.mean(y, axis=-1, keepdims=True)
    var = jnp.mean(jnp.square(y - mu), axis=-1, keepdims=True)
    z = (y - mu) * lax.rsqrt(var + EPS) * lng_ref[...] + lnb_ref[...]
    s = z * jax.nn.sigmoid(z)
    o_ref[0] = (s * lax.rsqrt(jnp.mean(s * s, axis=-1, keepdims=True) + EPS) * on_ref[...]).astype(o_ref.dtype)


def _conformer_conv(ag, conv_w, conv_b, ln_g, ln_b, out_norm, *, tt=256, lane_chunk=128):
    nb, seq, c2 = ag.shape
    c = c2 // 2
    hb = tt // CONV_HALO
    row = lambda v: v.reshape(1, c)
    return pl.pallas_call(
        functools.partial(_conv_kernel, lane_chunk=lane_chunk),
        out_shape=jax.ShapeDtypeStruct((nb, seq, c), BF16),
        grid=(nb, seq // tt),
        in_specs=[
            pl.BlockSpec((1, tt, c), lambda b, i: (b, i, 0)),
            pl.BlockSpec((1, tt, c), lambda b, i: (b, i, 1)),
            pl.BlockSpec((1, CONV_HALO, c), lambda b, i: (b, jnp.maximum(i * hb - 1, 0), 0)),
            pl.BlockSpec((1, CONV_HALO, c), lambda b, i: (b, jnp.maximum(i * hb - 1, 0), 1)),
            pl.BlockSpec((CONV_KERNEL, c), lambda b, i: (0, 0)),
            pl.BlockSpec((1, c), lambda b, i: (0, 0)),
            pl.BlockSpec((1, c), lambda b, i: (0, 0)),
            pl.BlockSpec((1, c), lambda b, i: (0, 0)),
            pl.BlockSpec((1, c), lambda b, i: (0, 0)),
        ],
        out_specs=pl.BlockSpec((1, tt, c), lambda b, i: (b, i, 0)),
        scratch_shapes=[pltpu.VMEM((CONV_HALO + tt, c), F32), pltpu.VMEM((tt, c), F32)],
        compiler_params=_cparams(2),
        name="conformer_conv",
    )(ag, ag, ag, ag, conv_w, row(conv_b), row(ln_g), row(ln_b), row(out_norm))


def _attn_kernel(qit_ref, w_ref, ki_ref, q_ref, kt_ref, v_ref, on_ref, o_ref,
                 key_scr, bias_scr, y_scr, *, topk, idx_scale, attn_scale):
    i = pl.program_id(1)
    n_att = lax.shift_right_logical(i * Q_BLOCK, int(math.log2(ATT_CHUNK))) + 1
    n_key = n_att * (ATT_CHUNK // KEY_CHUNK)
    n_heads = q_ref.shape[2] // HEAD_DIM
    group = n_heads // N_KV_HEADS
    q_pos = i * Q_BLOCK + lax.broadcasted_iota(jnp.int32, (KEY_CHUNK, Q_BLOCK), 1)
    row_iota = lax.broadcasted_iota(jnp.int32, (KEY_CHUNK, Q_BLOCK), 0)

    def score_body(kc, carry):
        r0 = pl.multiple_of(kc * KEY_CHUNK, KEY_CHUNK)
        ki_c = ki_ref[0, pl.ds(r0, KEY_CHUNK), :]
        acc = jnp.zeros((KEY_CHUNK, Q_BLOCK), F32)
        for hp in range(IDX_HEADS // 2):
            cols = slice(hp * 2 * Q_BLOCK, (hp + 1) * 2 * Q_BLOCK)
            s = jnp.dot(ki_c, qit_ref[0, 0, :, cols], preferred_element_type=F32)
            r = jnp.maximum(s, 0.0) * w_ref[0, 0, :, cols]
            acc = acc + (r[:, :Q_BLOCK] + r[:, Q_BLOCK:])
        sc = acc * idx_scale
        sc = jnp.where(sc == 0.0, 0.0, sc)
        bits = lax.bitcast_convert_type(sc, jnp.int32)
        key = jnp.where(bits < 0, bits ^ jnp.int32(0x7FFFFFFF), bits)
        key = jnp.where(r0 + row_iota <= q_pos, key, jnp.int32(INT_MIN))
        key_scr[pl.ds(r0, KEY_CHUNK), :] = key
        return carry

    lax.fori_loop(0, n_key, score_body, 0)

    def bit_body(it, thr):
        cand = jnp.where(it == 0, jnp.zeros_like(thr), thr | lax.shift_left(jnp.int32(1), 31 - it))

        def count_body(kc, acc):
            r0 = pl.multiple_of(kc * KEY_CHUNK, KEY_CHUNK)
            hit = jnp.where(key_scr[pl.ds(r0, KEY_CHUNK), :] >= cand, 1, 0)
            return acc + hit.reshape(KEY_CHUNK // SUBLANE, SUBLANE, Q_BLOCK).sum(axis=0)

        acc = lax.fori_loop(0, n_key, count_body, jnp.zeros((SUBLANE, Q_BLOCK), jnp.int32))
        cnt = acc.sum(axis=0, keepdims=True)
        return jnp.where(cnt >= topk, cand, thr)

    thr = lax.fori_loop(0, 32, bit_body, jnp.full((1, Q_BLOCK), INT_MIN, jnp.int32))
    thr = jnp.maximum(thr, jnp.int32(INT_MIN + 1))

    def bias_body(kc, carry):
        r0 = pl.multiple_of(kc * KEY_CHUNK, KEY_CHUNK)
        b_t = jnp.where(key_scr[pl.ds(r0, KEY_CHUNK), :] >= thr, 0.0, NEG_BIAS)
        bias_scr[kc] = b_t.T
        return carry

    lax.fori_loop(0, n_key, bias_body, 0)

    c_exp = attn_scale * math.log2(math.e)
    rows = group * Q_BLOCK
    halves = ATT_CHUNK // KEY_CHUNK
    for g in range(N_KV_HEADS):
        qg = jnp.concatenate(
            [q_ref[0, :, (g * group + j) * HEAD_DIM:(g * group + j + 1) * HEAD_DIM] for j in range(group)], axis=0)

        def att_body(c, carry, g=g, qg=qg):
            m, l, acc = carry
            s = jnp.dot(qg, kt_ref[0, c, g * HEAD_DIM:(g + 1) * HEAD_DIM, :], preferred_element_type=F32)
            bias = jnp.concatenate([bias_scr[c * halves + h] for h in range(halves)], axis=1)
            s = (s.reshape(group, Q_BLOCK, ATT_CHUNK) + bias[None]).reshape(rows, ATT_CHUNK)
            m_new = jnp.maximum(m, s.max(axis=1, keepdims=True))
            alpha = jnp.exp2((m - m_new) * c_exp)
            p = jnp.exp2((s - m_new) * c_exp)
            l = alpha * l + p.sum(axis=1, keepdims=True)
            r0 = pl.multiple_of(c * ATT_CHUNK, ATT_CHUNK)
            vv = v_ref[0, pl.ds(r0, ATT_CHUNK), g * HEAD_DIM:(g + 1) * HEAD_DIM]
            acc = alpha * acc + jnp.dot(p.astype(vv.dtype), vv, preferred_element_type=F32)
            return m_new, l, acc

        _, l, acc = lax.fori_loop(
            0, n_att, att_body,
            (jnp.full((rows, 1), M_INIT, F32), jnp.zeros((rows, 1), F32), jnp.zeros((rows, HEAD_DIM), F32)))
        o = acc / l
        for j in range(group):
            y_scr[:, (g * group + j) * HEAD_DIM:(g * group + j + 1) * HEAD_DIM] = o[j * Q_BLOCK:(j + 1) * Q_BLOCK]

    y = y_scr[...]
    o_ref[0] = (y * lax.rsqrt(jnp.mean(y * y, axis=-1, keepdims=True) + EPS) * on_ref[...]).astype(o_ref.dtype)


def _dsa_attention(qkvqi, kw, out_norm, attn_width):
    nb, seq, _ = qkvqi.shape
    kv_width = N_KV_HEADS * HEAD_DIM
    nblk = seq // Q_BLOCK
    topk = min(TOPK_MAX, seq // 4)
    idx_scale = (IDX_DIM ** -0.5) * (IDX_HEADS ** -0.5)
    attn_scale = HEAD_DIM ** -0.5
    k = qkvqi[:, :, attn_width:attn_width + kv_width]
    qi = qkvqi[:, :, attn_width + 2 * kv_width:]
    kt = k.reshape(nb, seq // ATT_CHUNK, ATT_CHUNK, kv_width).transpose(0, 1, 3, 2)
    qit = qi.reshape(nb, nblk, Q_BLOCK, IDX_HEADS, IDX_DIM).transpose(0, 1, 4, 3, 2)
    qit = qit.reshape(nb, nblk, IDX_DIM, IDX_HEADS * Q_BLOCK)
    ki = kw[:, :, :IDX_DIM].astype(BF16)
    wi = kw[:, :, IDX_DIM:IDX_DIM + IDX_HEADS]
    wrow = wi.reshape(nb, nblk, Q_BLOCK, IDX_HEADS).transpose(0, 1, 3, 2).reshape(nb, nblk, 1, IDX_HEADS * Q_BLOCK)
    v_blk = (attn_width + kv_width) // kv_width
    return pl.pallas_call(
        functools.partial(_attn_kernel, topk=topk, idx_scale=idx_scale, attn_scale=attn_scale),
        out_shape=jax.ShapeDtypeStruct((nb, seq, attn_width), BF16),
        grid=(nb, nblk),
        in_specs=[
            pl.BlockSpec((1, 1, IDX_DIM, IDX_HEADS * Q_BLOCK), lambda b, i: (b, i, 0, 0)),
            pl.BlockSpec((1, 1, 1, IDX_HEADS * Q_BLOCK), lambda b, i: (b, i, 0, 0)),
            pl.BlockSpec((1, seq, IDX_DIM), lambda b, i: (b, 0, 0)),
            pl.BlockSpec((1, Q_BLOCK, attn_width), lambda b, i: (b, i, 0)),
            pl.BlockSpec((1, seq // ATT_CHUNK, kv_width, ATT_CHUNK), lambda b, i: (b, 0, 0, 0)),
            pl.BlockSpec((1, seq, kv_width), lambda b, i: (b, 0, v_blk)),
            pl.BlockSpec((1, attn_width), lambda b, i: (0, 0)),
        ],
        out_specs=pl.BlockSpec((1, Q_BLOCK, attn_width), lambda b, i: (b, i, 0)),
        scratch_shapes=[
            pltpu.VMEM((seq, Q_BLOCK), jnp.int32),
            pltpu.VMEM((seq // KEY_CHUNK, Q_BLOCK, KEY_CHUNK), F32),
            pltpu.VMEM((Q_BLOCK, attn_width), F32),
        ],
        compiler_params=_cparams(2),
        name="dsa_attention",
    )(qit, wrow, ki, qkvqi, kt, qkvqi, out_norm.reshape(1, attn_width))


def _ffn(x2d, seq, norm_g, shift, scale, gate, wgu, wd):
    h = _norm_mod(x2d, norm_g, scale, shift, seq)
    act = _matmul_swiglu(h, wgu.astype(BF16))
    return _matmul_residual(act, wd.astype(BF16), x2d, gate, seq, coef=FFN_RES, bm=512, bn=512)


def kernel(x, c, ada_w, ada_b, ffn1_norm, ffn1_wgu, ffn1_wd, mix_norm, w_in, conv_w, conv_b, conv_ln_g,
           conv_ln_b, conv_out_norm, attn_out_norm, w_out, ffn2_norm, ffn2_wgu, ffn2_wd, final_norm):
    nb, seq, d = x.shape
    depth = ada_w.shape[0]
    conv_width = d // 2
    attn_width = d - conv_width
    kv_width = N_KV_HEADS * HEAD_DIM
    n_main = 2 * conv_width + attn_width + 2 * kv_width + IDX_HEADS * IDX_DIM
    n_tail = IDX_DIM + IDX_HEADS

    mod = _modulation(c, ada_w, ada_b)
    x2d = x.reshape(nb * seq, d)
    for l in range(depth):
        sh1, sc1, g1, sh2, sc2, g2, sh3, sc3, g3 = (mod[l, :, i * d:(i + 1) * d] for i in range(N_MOD))
        x2d = _ffn(x2d, seq, ffn1_norm[l], sh1, sc1, g1, ffn1_wgu[l], ffn1_wd[l])

        h = _norm_mod(x2d, mix_norm[l], sc2, sh2, seq)
        w_in_b = w_in[l].astype(BF16)
        w_tail = jnp.pad(w_in_b[:, n_main:n_main + n_tail], ((0, 0), (0, LANE - n_tail)))
        ag = _matmul(h, w_in_b, col0=0, ncols=2 * conv_width, out_dtype=F32, bm=1024, bn=512)
        qkvqi = _matmul(h, w_in_b, col0=2 * conv_width, ncols=n_main - 2 * conv_width, out_dtype=BF16,
                        bm=1024, bn=512)
        kw = _matmul(h, w_tail, col0=0, ncols=LANE, out_dtype=F32, bm=1024, bn=LANE)
        y_conv = _conformer_conv(ag.reshape(nb, seq, 2 * conv_width), conv_w[l], conv_b[l], conv_ln_g[l],
                                 conv_ln_b[l], conv_out_norm[l])
        y_attn = _dsa_attention(qkvqi.reshape(nb, seq, -1), kw.reshape(nb, seq, LANE), attn_out_norm[l],
                                attn_width)
        y = jnp.concatenate([y_conv, y_attn], axis=-1).reshape(nb * seq, d)
        x2d = _matmul_residual(y, w_out[l].astype(BF16), x2d, g2, seq, coef=1.0, bm=1024, bn=512)

        x2d = _ffn(x2d, seq, ffn2_norm[l], sh3, sc3, g3, ffn2_wgu[l], ffn2_wd[l])
    return _final_norm(x2d, final_norm).reshape(nb, seq, d)
```

```python
import functools
import math

import jax
import jax.numpy as jnp
from jax import lax
from jax.experimental import pallas as pl
from jax.experimental.pallas import tpu as pltpu

HEAD_DIM = 128
N_KV_HEADS = 4
IDX_HEADS = 32
IDX_DIM = 64
TOPK_MAX = 256
CONV_KERNEL = 31
Q_BLOCK = 128
N_MOD = 9
EPS = 1e-6
FFN_RES = 0.5

LANE = 128
SUBLANE = 8
KEY_CHUNK = 256
ATT_CHUNK = 512
ATT_STRIP = 32
CONV_HALO = 32
CONV_ROWS = 128
K_SPLIT = 4096
VMEM_LIMIT = 56 * 1024 * 1024

INT_MIN = -(2 ** 31)
INT_MAX = 2 ** 31 - 1
NEG_BIAS = -2e30
M_INIT = -1e30
MAX_BISECT = 40

F32 = jnp.float32
BF16 = jnp.bfloat16


def _cparams(n_axes):
    return pltpu.CompilerParams(dimension_semantics=("parallel",) * n_axes, vmem_limit_bytes=VMEM_LIMIT)


def _mod_kernel(cb_ref, w_ref, b_ref, o_ref, act_scr, *, k_chunk):
    nb, kdim, _ = cb_ref.shape
    tn = w_ref.shape[2]

    @pl.when(jnp.logical_and(pl.program_id(0) == 0, pl.program_id(1) == 0))
    def _():
        cb = cb_ref[...]
        act_scr[...] = cb * jax.nn.sigmoid(cb)

    def body(kc, accs):
        k0 = pl.multiple_of(kc * k_chunk, k_chunk)
        w = w_ref[0, pl.ds(k0, k_chunk), :]
        out = []
        for b in range(nb):
            p = w * jnp.tile(act_scr[b, pl.ds(k0, k_chunk), :], (1, tn // LANE))
            out.append(accs[b] + p.reshape(k_chunk // SUBLANE, SUBLANE, tn).sum(axis=0))
        return tuple(out)

    accs = lax.fori_loop(0, kdim // k_chunk, body,
                         tuple(jnp.zeros((SUBLANE, tn), F32) for _ in range(nb)))
    for b in range(nb):
        o_ref[0, b:b + 1, :] = accs[b].sum(axis=0, keepdims=True) + b_ref[0]


def _modulation(c, ada_w, ada_b, *, tn=512, k_chunk=128):
    nl, kdim, n = ada_w.shape
    nb = c.shape[0]
    cb = jnp.broadcast_to(c[:, :, None], (nb, kdim, LANE))
    return pl.pallas_call(
        functools.partial(_mod_kernel, k_chunk=k_chunk),
        out_shape=jax.ShapeDtypeStruct((nl, nb, n), F32),
        grid=(nl, n // tn),
        in_specs=[
            pl.BlockSpec((nb, kdim, LANE), lambda l, j: (0, 0, 0)),
            pl.BlockSpec((1, kdim, tn), lambda l, j: (l, 0, j)),
            pl.BlockSpec((1, 1, tn), lambda l, j: (l, 0, j)),
        ],
        out_specs=pl.BlockSpec((1, nb, tn), lambda l, j: (l, 0, j)),
        scratch_shapes=[pltpu.VMEM((nb, kdim, LANE), F32)],
        compiler_params=pltpu.CompilerParams(dimension_semantics=("arbitrary", "arbitrary"),
                                             vmem_limit_bytes=VMEM_LIMIT),
        name="adaln_mod",
    )(cb, ada_w, ada_b.reshape(nl, 1, n))


def _norm_mod_kernel(x_ref, g_ref, sc_ref, sh_ref, o_ref):
    x = x_ref[...]
    y = x * lax.rsqrt(jnp.mean(x * x, axis=-1, keepdims=True) + EPS) * g_ref[0]
    o_ref[...] = (y * (1.0 + sc_ref[0]) + sh_ref[0]).astype(o_ref.dtype)


def _norm_mod(x2d, gains, layer, scale, shift, seq, *, tm=512):
    m, d = x2d.shape
    nb = scale.shape[0]
    per_b = seq // tm
    return pl.pallas_call(
        _norm_mod_kernel,
        out_shape=jax.ShapeDtypeStruct((m, d), BF16),
        grid=(m // tm,),
        in_specs=[
            pl.BlockSpec((tm, d), lambda i: (i, 0)),
            pl.BlockSpec((1, 1, d), lambda i: (layer, 0, 0)),
            pl.BlockSpec((1, 1, d), lambda i: (i // per_b, 0, 0)),
            pl.BlockSpec((1, 1, d), lambda i: (i // per_b, 0, 0)),
        ],
        out_specs=pl.BlockSpec((tm, d), lambda i: (i, 0)),
        compiler_params=_cparams(1),
        name="norm_mod",
    )(x2d, gains.reshape(-1, 1, d), scale.reshape(nb, 1, d), shift.reshape(nb, 1, d))


def _rmsnorm_kernel(x_ref, g_ref, o_ref):
    x = x_ref[...]
    o_ref[...] = x * lax.rsqrt(jnp.mean(x * x, axis=-1, keepdims=True) + EPS) * g_ref[...]


def _final_norm(x2d, gain, *, tm=512):
    m, d = x2d.shape
    return pl.pallas_call(
        _rmsnorm_kernel,
        out_shape=jax.ShapeDtypeStruct((m, d), F32),
        grid=(m // tm,),
        in_specs=[pl.BlockSpec((tm, d), lambda i: (i, 0)), pl.BlockSpec((1, d), lambda i: (0, 0))],
        out_specs=pl.BlockSpec((tm, d), lambda i: (i, 0)),
        compiler_params=_cparams(1),
        name="final_norm",
    )(x2d, gain.reshape(1, d))


def _mm_kernel(a_ref, w_ref, o_ref):
    o_ref[...] = jnp.dot(a_ref[...], w_ref[0].astype(BF16), preferred_element_type=F32).astype(o_ref.dtype)


def _matmul(a, w, layer, *, col0, ncols, out_dtype, bm, bn):
    m, k = a.shape
    off = col0 // bn
    return pl.pallas_call(
        _mm_kernel,
        out_shape=jax.ShapeDtypeStruct((m, ncols), out_dtype),
        grid=(m // bm, ncols // bn),
        in_specs=[pl.BlockSpec((bm, k), lambda i, j: (i, 0)),
                  pl.BlockSpec((1, k, bn), lambda i, j: (layer, 0, j + off))],
        out_specs=pl.BlockSpec((bm, bn), lambda i, j: (i, j)),
        compiler_params=_cparams(2),
        name="matmul",
    )(a, w)


def _mm_swiglu_kernel(a_ref, wg_ref, wu_ref, o_ref):
    a = a_ref[...]
    g = jnp.dot(a, wg_ref[0].astype(BF16), preferred_element_type=F32)
    u = jnp.dot(a, wu_ref[0].astype(BF16), preferred_element_type=F32)
    o_ref[...] = (g * jax.nn.sigmoid(g) * u).astype(o_ref.dtype)


def _matmul_swiglu(a, wgu, layer, *, bm=1024, bn=256):
    m, k = a.shape
    f = wgu.shape[2] // 2
    nj = f // bn
    return pl.pallas_call(
        _mm_swiglu_kernel,
        out_shape=jax.ShapeDtypeStruct((m, f), BF16),
        grid=(m // bm, nj),
        in_specs=[pl.BlockSpec((bm, k), lambda i, j: (i, 0)),
                  pl.BlockSpec((1, k, bn), lambda i, j: (layer, 0, j)),
                  pl.BlockSpec((1, k, bn), lambda i, j: (layer, 0, j + nj))],
        out_specs=pl.BlockSpec((bm, bn), lambda i, j: (i, j)),
        compiler_params=_cparams(2),
        name="matmul_swiglu",
    )(a, wgu, wgu)


def _mm_res_kernel(a_ref, w_ref, x_ref, gate_ref, o_ref, acc_ref, *, coef, nk):
    k = pl.program_id(1)
    j = pl.program_id(2)
    y = jnp.dot(a_ref[...], w_ref[0].astype(BF16), preferred_element_type=F32)
    if nk > 1:
        @pl.when(jnp.logical_and(pl.program_id(0) == 0, jnp.logical_and(k == 0, j == 0)))
        def _():
            acc_ref[...] = jnp.zeros_like(acc_ref)

        y = y + jnp.where(k > 0, acc_ref[j], 0.0)
        acc_ref[j] = y
    o_ref[...] = x_ref[...] + (coef * gate_ref[0]) * y


def _matmul_residual(a, w, layer, x2d, gate, seq, *, coef, bm=1024, bn=256):
    m, k = a.shape
    n = w.shape[2]
    nb = gate.shape[0]
    bk = min(k, K_SPLIT)
    nk = k // bk
    nj = n // bn
    per_b = seq // bm
    last = lambda kk, j: jnp.where(kk == nk - 1, j, 0)
    return pl.pallas_call(
        functools.partial(_mm_res_kernel, coef=coef, nk=nk),
        out_shape=jax.ShapeDtypeStruct((m, n), F32),
        grid=(m // bm, nk, nj),
        in_specs=[pl.BlockSpec((bm, bk), lambda i, kk, j: (i, kk)),
                  pl.BlockSpec((1, bk, bn), lambda i, kk, j: (layer, kk, j)),
                  pl.BlockSpec((bm, bn), lambda i, kk, j: (i, last(kk, j))),
                  pl.BlockSpec((1, 1, bn), lambda i, kk, j: (i // per_b, 0, j))],
        out_specs=pl.BlockSpec((bm, bn), lambda i, kk, j: (i, last(kk, j))),
        scratch_shapes=[pltpu.VMEM((nj if nk > 1 else 1, bm if nk > 1 else SUBLANE, bn if nk > 1 else LANE), F32)],
        compiler_params=pltpu.CompilerParams(dimension_semantics=("arbitrary", "arbitrary", "arbitrary"),
                                             vmem_limit_bytes=VMEM_LIMIT),
        name="matmul_residual",
    )(a, w, x2d, gate.reshape(nb, 1, n))


def _mm_res2_kernel(a1_ref, a2_ref, w1_ref, w2_ref, x_ref, gate_ref, o_ref, *, coef):
    y = jnp.dot(a1_ref[...], w1_ref[0].astype(BF16), preferred_element_type=F32)
    y = y + jnp.dot(a2_ref[...], w2_ref[0].astype(BF16), preferred_element_type=F32)
    o_ref[...] = x_ref[...] + (coef * gate_ref[0]) * y


def _matmul_residual2(a1, a2, w, layer, x2d, gate, seq, *, coef, bm=1024, bn=256):
    m, k1 = a1.shape
    k2 = a2.shape[1]
    assert k1 == k2
    n = w.shape[2]
    nb = gate.shape[0]
    per_b = seq // bm
    return pl.pallas_call(
        functools.partial(_mm_res2_kernel, coef=coef),
        out_shape=jax.ShapeDtypeStruct((m, n), F32),
        grid=(m // bm, n // bn),
        in_specs=[pl.BlockSpec((bm, k1), lambda i, j: (i, 0)),
                  pl.BlockSpec((bm, k2), lambda i, j: (i, 0)),
                  pl.BlockSpec((1, k1, bn), lambda i, j: (layer, 0, j)),
                  pl.BlockSpec((1, k2, bn), lambda i, j: (layer, 1, j)),
                  pl.BlockSpec((bm, bn), lambda i, j: (i, j)),
                  pl.BlockSpec((1, 1, bn), lambda i, j: (i // per_b, 0, j))],
        out_specs=pl.BlockSpec((bm, bn), lambda i, j: (i, j)),
        compiler_params=_cparams(2),
        name="matmul_residual2",
    )(a1, a2, w, w, x2d, gate.reshape(nb, 1, n))


def _conv_kernel(a_ref, g_ref, ap_ref, gp_ref, cw_ref, cb_ref, lng_ref, lnb_ref, on_ref, o_ref,
                 u_scr, y_scr):
    i = pl.program_id(1)
    tt, c = a_ref.shape[1], a_ref.shape[2]
    u_prev = ap_ref[0] * jax.nn.sigmoid(gp_ref[0])
    u_scr[0:CONV_HALO, :] = jnp.where(i > 0, u_prev, 0.0)
    u_scr[CONV_HALO:CONV_HALO + tt, :] = a_ref[0] * jax.nn.sigmoid(g_ref[0])
    base = CONV_HALO - (CONV_KERNEL - 1)
    taps = {}
    for j in range(CONV_KERNEL):
        taps.setdefault((base + j) % SUBLANE, []).append(j)
    for c0 in range(0, c, LANE):
        cols = slice(c0, c0 + LANE)
        for t0 in range(0, tt, CONV_ROWS):
            acc = jnp.broadcast_to(cb_ref[0, :, cols], (CONV_ROWS, LANE))
            for r, js in taps.items():
                span = (base + max(js)) // SUBLANE * SUBLANE + CONV_ROWS
                if r:
                    win = u_scr[t0:t0 + span + SUBLANE, cols]
                    win = pltpu.roll(win, span + SUBLANE - r, axis=0)
                else:
                    win = u_scr[t0:t0 + span, cols]
                for j in js:
                    q = (base + j) // SUBLANE * SUBLANE
                    acc = acc + cw_ref[j:j + 1, cols] * win[q:q + CONV_ROWS]
            y_scr[t0:t0 + CONV_ROWS, cols] = acc
    y = y_scr[...]
    mu = jnp.mean(y, axis=-1, keepdims=True)
    var = jnp.mean(jnp.square(y - mu), axis=-1, keepdims=True)
    z = (y - mu) * lax.rsqrt(var + EPS) * lng_ref[0] + lnb_ref[0]
    s = z * jax.nn.sigmoid(z)
    o_ref[0] = (s * lax.rsqrt(jnp.mean(s * s, axis=-1, keepdims=True) + EPS) * on_ref[0]).astype(o_ref.dtype)


def _conformer_conv(ag, layer, conv_w, conv_b, ln_g, ln_b, out_norm, *, tt=256):
    nb, seq, c2 = ag.shape
    c = c2 // 2
    hb = tt // CONV_HALO
    vec = lambda v: v.reshape(-1, 1, c)
    vec_spec = pl.BlockSpec((1, 1, c), lambda b, i: (layer, 0, 0))
    return pl.pallas_call(
        _conv_kernel,
        out_shape=jax.ShapeDtypeStruct((nb, seq, c), BF16),
        grid=(nb, seq // tt),
        in_specs=[
            pl.BlockSpec((1, tt, c), lambda b, i: (b, i, 0)),
            pl.BlockSpec((1, tt, c), lambda b, i: (b, i, 1)),
            pl.BlockSpec((1, CONV_HALO, c), lambda b, i: (b, jnp.maximum(i * hb - 1, 0), 0)),
            pl.BlockSpec((1, CONV_HALO, c), lambda b, i: (b, jnp.maximum(i * hb - 1, 0), 1)),
            pl.BlockSpec((None, CONV_KERNEL, c), lambda b, i: (layer, 0, 0)),
            vec_spec, vec_spec, vec_spec, vec_spec,
        ],
        out_specs=pl.BlockSpec((1, tt, c), lambda b, i: (b, i, 0)),
        scratch_shapes=[pltpu.VMEM((CONV_HALO + tt, c), F32), pltpu.VMEM((tt, c), F32)],
        compiler_params=_cparams(2),
        name="conformer_conv",
    )(ag, ag, ag, ag, conv_w, vec(conv_b), vec(ln_g), vec(ln_b), vec(out_norm))


def _attn_kernel(qit_ref, w_ref, ki_ref, q_ref, kt_ref, v_ref, on_ref, o_ref,
                 key_scr, bias_scr, y_scr, s_scr, p_scr, m_scr, a_scr, acc_scr, *, topk, idx_scale, attn_scale):
    i = pl.program_id(1)
    n_att = lax.shift_right_logical(i * Q_BLOCK, int(math.log2(ATT_CHUNK))) + 1
    halves = ATT_CHUNK // KEY_CHUNK
    n_key = n_att * halves
    n_heads = q_ref.shape[2] // HEAD_DIM
    group = n_heads // N_KV_HEADS
    i32 = jnp.int32
    q_pos = i * Q_BLOCK + lax.broadcasted_iota(i32, (KEY_CHUNK, Q_BLOCK), 1)
    row_iota = lax.broadcasted_iota(i32, (KEY_CHUNK, Q_BLOCK), 0)
    att_rows = lax.broadcasted_iota(i32, (ATT_CHUNK, Q_BLOCK), 0)

    def fold(x):
        return x.reshape(ATT_CHUNK // SUBLANE, SUBLANE, Q_BLOCK).sum(axis=0)

    def score_body(kc, kmax):
        r0 = pl.multiple_of(kc * KEY_CHUNK, KEY_CHUNK)
        ki_c = ki_ref[0, pl.ds(r0, KEY_CHUNK), :]
        acc = jnp.zeros((KEY_CHUNK, Q_BLOCK), F32)
        for hp in range(IDX_HEADS // 2):
            cols = slice(hp * 2 * Q_BLOCK, (hp + 1) * 2 * Q_BLOCK)
            s = jnp.dot(ki_c, qit_ref[0, 0, :, cols], preferred_element_type=F32)
            r = jnp.maximum(s, 0.0) * w_ref[0, 0, :, cols]
            acc = acc + (r[:, :Q_BLOCK] + r[:, Q_BLOCK:])
        sc = acc * idx_scale
        sc = jnp.where(sc == 0.0, 0.0, sc)
        bits = lax.bitcast_convert_type(sc, i32)
        key = jnp.where(bits < 0, bits ^ i32(INT_MAX), bits)
        key = jnp.where(r0 + row_iota <= q_pos, key, i32(INT_MIN))
        key_scr[pl.ds(r0, KEY_CHUNK), :] = key
        return jnp.maximum(kmax, key.reshape(KEY_CHUNK // SUBLANE, SUBLANE, Q_BLOCK).max(axis=0))

    kmax = lax.fori_loop(0, n_key, score_body, jnp.full((SUBLANE, Q_BLOCK), INT_MIN, i32))
    kmax = kmax.max(axis=0, keepdims=True)

    def count(pred):
        def body(c, acc):
            r0 = pl.multiple_of(c * ATT_CHUNK, ATT_CHUNK)
            return acc + fold(jnp.where(pred(key_scr[pl.ds(r0, ATT_CHUNK), :], r0), 1, 0))
        return lax.fori_loop(0, n_att, body, jnp.zeros((SUBLANE, Q_BLOCK), i32)).sum(axis=0, keepdims=True)

    def bis_cond(st):
        return jnp.logical_and(st[3] > 0, st[4] < MAX_BISECT)

    def bis_body(st):
        lo, hi, cnt_lo, _, it = st
        mid = (lo >> 1) + (hi >> 1) + (lo & hi & 1)
        cnt = count(lambda k, r0: k >= mid)
        ge = cnt >= topk
        eq = cnt == topk
        lo_n = jnp.where(ge, mid, lo)
        hi_n = jnp.where(eq, mid + 1, jnp.where(ge, hi, mid))
        cnt_n = jnp.where(ge, cnt, cnt_lo)
        active = jnp.max(jnp.where(hi_n - 1 > lo_n, 1, 0))
        return lo_n, hi_n, cnt_n, active, it + 1

    lo0 = jnp.full((1, Q_BLOCK), INT_MIN, i32)
    hi0 = jnp.where(kmax == INT_MAX, kmax, kmax + 1)
    cnt0 = jnp.full((1, Q_BLOCK), 1, i32) * (n_att * ATT_CHUNK)
    thr, _, cnt_thr, _, _ = lax.while_loop(bis_cond, bis_body, (lo0, hi0, cnt0, i32(1), i32(0)))

    excess = jnp.where(thr > INT_MIN, cnt_thr - topk, 0)

    @pl.when(jnp.max(excess) > 0)
    def _():
        need = topk - count(lambda k, r0: k > thr)

        def pos_body(_, st):
            lo_p, hi_p = st
            mid = (lo_p + hi_p) >> 1
            ok = count(lambda k, r0: jnp.logical_and(k == thr, r0 + att_rows <= mid)) >= need
            return jnp.where(ok, lo_p, mid), jnp.where(ok, mid, hi_p)

        n_rows = n_att * ATT_CHUNK
        _, cut = lax.fori_loop(0, int(math.log2(key_scr.shape[0])) + 1, pos_body,
                               (jnp.full((1, Q_BLOCK), -1, i32), jnp.full((1, Q_BLOCK), 1, i32) * (n_rows - 1)))
        cut = jnp.where(excess > 0, cut, INT_MAX)

        def demote_body(c, carry):
            r0 = pl.multiple_of(c * ATT_CHUNK, ATT_CHUNK)
            k = key_scr[pl.ds(r0, ATT_CHUNK), :]
            drop = jnp.logical_and(k == thr, r0 + att_rows > cut)
            key_scr[pl.ds(r0, ATT_CHUNK), :] = jnp.where(drop, k - 1, k)
            return carry

        lax.fori_loop(0, n_att, demote_body, 0)

    thr = jnp.maximum(thr, i32(INT_MIN + 1))

    def bias_body(kc, carry):
        r0 = pl.multiple_of(kc * KEY_CHUNK, KEY_CHUNK)
        b_t = jnp.where(key_scr[pl.ds(r0, KEY_CHUNK), :] >= thr, 0.0, NEG_BIAS)
        bias_scr[kc] = b_t.T
        return carry

    lax.fori_loop(0, n_key, bias_body, 0)

    c_exp = attn_scale * math.log2(math.e)
    rows = group * Q_BLOCK
    m_scr[...] = jnp.full(m_scr.shape, M_INIT, F32)
    acc_scr[...] = jnp.zeros(acc_scr.shape, F32)

    def att_body(c, carry):
        r0 = pl.multiple_of(c * ATT_CHUNK, ATT_CHUNK)
        bias = jnp.concatenate([bias_scr[c * halves + h] for h in range(halves)], axis=1)
        for g in range(N_KV_HEADS):
            qg = jnp.concatenate(
                [q_ref[0, :, (g * group + j) * HEAD_DIM:(g * group + j + 1) * HEAD_DIM] for j in range(group)],
                axis=0)
            s = jnp.dot(qg, kt_ref[0, c, g * HEAD_DIM:(g + 1) * HEAD_DIM, :], preferred_element_type=F32)
            s_scr[g] = (s.reshape(group, Q_BLOCK, ATT_CHUNK) + bias[None]).reshape(rows, ATT_CHUNK)
        for g in range(N_KV_HEADS):
            for t0 in range(0, rows, ATT_STRIP):
                strip = slice(t0, t0 + ATT_STRIP)
                x = s_scr[g, strip, :]
                m_old = m_scr[g, strip, :]
                m_new = jnp.maximum(m_old, x.max(axis=1, keepdims=True))
                a_scr[g, strip, :] = jnp.exp2((m_old - m_new) * c_exp)
                m_scr[g, strip, :] = m_new
                p = jnp.exp2((x - jnp.tile(m_new, (1, ATT_CHUNK // LANE))) * c_exp)
                p_scr[g, strip, :] = p.astype(p_scr.dtype)
        for g in range(N_KV_HEADS):
            vv = v_ref[0, pl.ds(r0, ATT_CHUNK), g * HEAD_DIM:(g + 1) * HEAD_DIM]
            v_ones = jnp.concatenate([vv, jnp.ones_like(vv)], axis=1)
            pv = jnp.dot(p_scr[g], v_ones, preferred_element_type=F32)
            acc_scr[g] = jnp.tile(a_scr[g], (1, 2)) * acc_scr[g] + pv
        return carry

    lax.fori_loop(0, n_att, att_body, 0)
    for g in range(N_KV_HEADS):
        acc = acc_scr[g]
        o = acc[:, :HEAD_DIM] / acc[:, HEAD_DIM:]
        for j in range(group):
            y_scr[:, (g * group + j) * HEAD_DIM:(g * group + j + 1) * HEAD_DIM] = o[j * Q_BLOCK:(j + 1) * Q_BLOCK]

    y = y_scr[...]
    o_ref[0] = (y * lax.rsqrt(jnp.mean(y * y, axis=-1, keepdims=True) + EPS) * on_ref[0]).astype(o_ref.dtype)


def _dsa_attention(qkvqi, kw, out_norm, layer, attn_width):
    nb, seq, _ = qkvqi.shape
    kv_width = N_KV_HEADS * HEAD_DIM
    nblk = seq // Q_BLOCK
    topk = min(TOPK_MAX, seq // 4)
    idx_scale = (IDX_DIM ** -0.5) * (IDX_HEADS ** -0.5)
    attn_scale = HEAD_DIM ** -0.5
    k = qkvqi[:, :, attn_width:attn_width + kv_width]
    qi = qkvqi[:, :, attn_width + 2 * kv_width:]
    kt = k.reshape(nb, seq // ATT_CHUNK, ATT_CHUNK, kv_width).transpose(0, 1, 3, 2)
    qit = qi.reshape(nb, nblk, Q_BLOCK, IDX_HEADS, IDX_DIM).transpose(0, 1, 4, 3, 2)
    qit = qit.reshape(nb, nblk, IDX_DIM, IDX_HEADS * Q_BLOCK)
    ki = kw[:, :, :IDX_DIM].astype(BF16)
    wi = kw[:, :, IDX_DIM:IDX_DIM + IDX_HEADS]
    wrow = wi.reshape(nb, nblk, Q_BLOCK, IDX_HEADS).transpose(0, 1, 3, 2).reshape(nb, nblk, 1, IDX_HEADS * Q_BLOCK)
    v_blk = (attn_width + kv_width) // kv_width
    rows = attn_width // kv_width * Q_BLOCK
    return pl.pallas_call(
        functools.partial(_attn_kernel, topk=topk, idx_scale=idx_scale, attn_scale=attn_scale),
        out_shape=jax.ShapeDtypeStruct((nb, seq, attn_width), BF16),
        grid=(nb, nblk),
        in_specs=[
            pl.BlockSpec((1, 1, IDX_DIM, IDX_HEADS * Q_BLOCK), lambda b, i: (b, i, 0, 0)),
            pl.BlockSpec((1, 1, 1, IDX_HEADS * Q_BLOCK), lambda b, i: (b, i, 0, 0)),
            pl.BlockSpec((1, seq, IDX_DIM), lambda b, i: (b, 0, 0)),
            pl.BlockSpec((1, Q_BLOCK, attn_width), lambda b, i: (b, i, 0)),
            pl.BlockSpec((1, seq // ATT_CHUNK, kv_width, ATT_CHUNK), lambda b, i: (b, 0, 0, 0)),
            pl.BlockSpec((1, seq, kv_width), lambda b, i: (b, 0, v_blk)),
            pl.BlockSpec((1, 1, attn_width), lambda b, i: (layer, 0, 0)),
        ],
        out_specs=pl.BlockSpec((1, Q_BLOCK, attn_width), lambda b, i: (b, i, 0)),
        scratch_shapes=[
            pltpu.VMEM((seq, Q_BLOCK), jnp.int32),
            pltpu.VMEM((seq // KEY_CHUNK, Q_BLOCK, KEY_CHUNK), F32),
            pltpu.VMEM((Q_BLOCK, attn_width), F32),
            pltpu.VMEM((N_KV_HEADS, rows, ATT_CHUNK), F32),
            pltpu.VMEM((N_KV_HEADS, rows, ATT_CHUNK), BF16),
            pltpu.VMEM((N_KV_HEADS, rows, LANE), F32),
            pltpu.VMEM((N_KV_HEADS, rows, LANE), F32),
            pltpu.VMEM((N_KV_HEADS, rows, 2 * HEAD_DIM), F32),
        ],
        compiler_params=_cparams(2),
        name="dsa_attention",
    )(qit, wrow, ki, qkvqi, kt, qkvqi, out_norm.reshape(-1, 1, attn_width))


def _ffn(x2d, seq, layer, norm_g, shift, scale, gate, wgu, wd):
    h = _norm_mod(x2d, norm_g, layer, scale, shift, seq)
    act = _matmul_swiglu(h, wgu, layer)
    return _matmul_residual(act, wd, layer, x2d, gate, seq, coef=FFN_RES)


def kernel(x, c, ada_w, ada_b, ffn1_norm, ffn1_wgu, ffn1_wd, mix_norm, w_in, conv_w, conv_b, conv_ln_g,
           conv_ln_b, conv_out_norm, attn_out_norm, w_out, ffn2_norm, ffn2_wgu, ffn2_wd, final_norm):
    nb, seq, d = x.shape
    depth = ada_w.shape[0]
    conv_width = d // 2
    attn_width = d - conv_width
    kv_width = N_KV_HEADS * HEAD_DIM
    n_main = 2 * conv_width + attn_width + 2 * kv_width + IDX_HEADS * IDX_DIM
    n_tail = IDX_DIM + IDX_HEADS
    w_tail = jnp.pad(w_in[:, :, n_main:n_main + n_tail], ((0, 0), (0, 0), (0, LANE - n_tail)))

    mod = _modulation(c, ada_w, ada_b)
    x2d = x.reshape(nb * seq, d)
    for l in range(depth):
        sh1, sc1, g1, sh2, sc2, g2, sh3, sc3, g3 = (mod[l, :, i * d:(i + 1) * d] for i in range(N_MOD))
        x2d = _ffn(x2d, seq, l, ffn1_norm, sh1, sc1, g1, ffn1_wgu, ffn1_wd)

        h = _norm_mod(x2d, mix_norm, l, sc2, sh2, seq)
        ag = _matmul(h, w_in, l, col0=0, ncols=2 * conv_width, out_dtype=F32, bm=1024, bn=256)
        qkvqi = _matmul(h, w_in, l, col0=2 * conv_width, ncols=n_main - 2 * conv_width, out_dtype=BF16,
                        bm=1024, bn=256)
        kw = _matmul(h, w_tail, l, col0=0, ncols=LANE, out_dtype=F32, bm=1024, bn=LANE)
        y_conv = _conformer_conv(ag.reshape(nb, seq, 2 * conv_width), l, conv_w, conv_b, conv_ln_g,
                                 conv_ln_b, conv_out_norm)
        y_attn = _dsa_attention(qkvqi.reshape(nb, seq, -1), kw.reshape(nb, seq, LANE), attn_out_norm, l,
                                attn_width)
        x2d = _matmul_residual2(y_conv.reshape(nb * seq, conv_width), y_attn.reshape(nb * seq, attn_width),
                                w_out, l, x2d, g2, seq, coef=1.0)

        x2d = _ffn(x2d, seq, l, ffn2_norm, sh3, sc3, g3, ffn2_wgu, ffn2_wd)
    return _final_norm(x2d, final_norm).reshape(nb, seq, d)
```

```python
import functools
import math

import jax
import jax.numpy as jnp
from jax import lax
from jax.experimental import pallas as pl
from jax.experimental.pallas import tpu as pltpu

HEAD_DIM = 128
N_KV_HEADS = 4
IDX_HEADS = 32
IDX_DIM = 64
TOPK_MAX = 256
CONV_KERNEL = 31
Q_BLOCK = 128
N_MOD = 9
EPS = 1e-6
FFN_RES = 0.5

LANE = 128
SUBLANE = 8
KEY_CHUNK = 256
ATT_CHUNK = 512
ATT_STRIP = 32
CONV_HALO = 32
CONV_ROWS = 128
VMEM_LIMIT = 56 * 1024 * 1024

INT_MIN = -(2 ** 31)
INT_MAX = 2 ** 31 - 1
NEG_BIAS = -2e30
M_INIT = -1e30
MAX_BISECT = 40
BISECT_UNROLL = 8

F32 = jnp.float32
BF16 = jnp.bfloat16


def _cparams(n_axes):
    return pltpu.CompilerParams(dimension_semantics=("parallel",) * n_axes, vmem_limit_bytes=VMEM_LIMIT)


def _mod_kernel(cb_ref, w_ref, b_ref, o_ref, act_scr, *, k_chunk):
    nb, kdim, _ = cb_ref.shape
    tn = w_ref.shape[2]

    @pl.when(jnp.logical_and(pl.program_id(0) == 0, pl.program_id(1) == 0))
    def _():
        cb = cb_ref[...]
        act_scr[...] = cb * jax.nn.sigmoid(cb)

    def body(kc, accs):
        k0 = pl.multiple_of(kc * k_chunk, k_chunk)
        w = w_ref[0, pl.ds(k0, k_chunk), :]
        out = []
        for b in range(nb):
            p = w * jnp.tile(act_scr[b, pl.ds(k0, k_chunk), :], (1, tn // LANE))
            out.append(accs[b] + p.reshape(k_chunk // SUBLANE, SUBLANE, tn).sum(axis=0))
        return tuple(out)

    accs = lax.fori_loop(0, kdim // k_chunk, body,
                         tuple(jnp.zeros((SUBLANE, tn), F32) for _ in range(nb)))
    for b in range(nb):
        o_ref[0, b:b + 1, :] = accs[b].sum(axis=0, keepdims=True) + b_ref[0]


def _modulation(c, ada_w, ada_b, *, tn=512, k_chunk=128):
    nl, kdim, n = ada_w.shape
    nb = c.shape[0]
    cb = jnp.broadcast_to(c[:, :, None], (nb, kdim, LANE))
    return pl.pallas_call(
        functools.partial(_mod_kernel, k_chunk=k_chunk),
        out_shape=jax.ShapeDtypeStruct((nl, nb, n), F32),
        grid=(nl, n // tn),
        in_specs=[
            pl.BlockSpec((nb, kdim, LANE), lambda l, j: (0, 0, 0)),
            pl.BlockSpec((1, kdim, tn), lambda l, j: (l, 0, j)),
            pl.BlockSpec((1, 1, tn), lambda l, j: (l, 0, j)),
        ],
        out_specs=pl.BlockSpec((1, nb, tn), lambda l, j: (l, 0, j)),
        scratch_shapes=[pltpu.VMEM((nb, kdim, LANE), F32)],
        compiler_params=pltpu.CompilerParams(dimension_semantics=("arbitrary", "arbitrary"),
                                             vmem_limit_bytes=VMEM_LIMIT),
        name="adaln_mod",
    )(cb, ada_w, ada_b.reshape(nl, 1, n))


def _norm_mod_kernel(x_ref, g_ref, sc_ref, sh_ref, o_ref):
    x = x_ref[...]
    y = x * lax.rsqrt(jnp.mean(x * x, axis=-1, keepdims=True) + EPS) * g_ref[0]
    o_ref[...] = (y * (1.0 + sc_ref[0]) + sh_ref[0]).astype(o_ref.dtype)


def _norm_mod(x2d, gains, layer, scale, shift, seq, *, tm=512):
    m, d = x2d.shape
    nb = scale.shape[0]
    per_b = seq // tm
    return pl.pallas_call(
        _norm_mod_kernel,
        out_shape=jax.ShapeDtypeStruct((m, d), BF16),
        grid=(m // tm,),
        in_specs=[
            pl.BlockSpec((tm, d), lambda i: (i, 0)),
            pl.BlockSpec((1, 1, d), lambda i: (layer, 0, 0)),
            pl.BlockSpec((1, 1, d), lambda i: (i // per_b, 0, 0)),
            pl.BlockSpec((1, 1, d), lambda i: (i // per_b, 0, 0)),
        ],
        out_specs=pl.BlockSpec((tm, d), lambda i: (i, 0)),
        compiler_params=_cparams(1),
        name="norm_mod",
    )(x2d, gains.reshape(-1, 1, d), scale.reshape(nb, 1, d), shift.reshape(nb, 1, d))


def _rmsnorm_kernel(x_ref, g_ref, o_ref):
    x = x_ref[...]
    o_ref[...] = x * lax.rsqrt(jnp.mean(x * x, axis=-1, keepdims=True) + EPS) * g_ref[...]


def _final_norm(x2d, gain, *, tm=512):
    m, d = x2d.shape
    return pl.pallas_call(
        _rmsnorm_kernel,
        out_shape=jax.ShapeDtypeStruct((m, d), F32),
        grid=(m // tm,),
        in_specs=[pl.BlockSpec((tm, d), lambda i: (i, 0)), pl.BlockSpec((1, d), lambda i: (0, 0))],
        out_specs=pl.BlockSpec((tm, d), lambda i: (i, 0)),
        compiler_params=_cparams(1),
        name="final_norm",
    )(x2d, gain.reshape(1, d))


def _mm_nt_kernel(a_ref, wt_ref, o_ref):
    y = lax.dot_general(a_ref[...], wt_ref[0].astype(BF16), (((1,), (1,)), ((), ())), preferred_element_type=F32)
    o_ref[...] = y.astype(o_ref.dtype)


def _matmul_nt(a, wt, layer, *, col0, ncols, out_dtype, bm, bn):
    m, k = a.shape
    off = col0 // bn
    return pl.pallas_call(
        _mm_nt_kernel,
        out_shape=jax.ShapeDtypeStruct((m, ncols), out_dtype),
        grid=(m // bm, ncols // bn),
        in_specs=[pl.BlockSpec((bm, k), lambda i, j: (i, 0)),
                  pl.BlockSpec((1, bn, k), lambda i, j: (layer, j + off, 0))],
        out_specs=pl.BlockSpec((bm, bn), lambda i, j: (i, j)),
        compiler_params=_cparams(2),
        name="matmul_nt",
    )(a, wt)


def _mm_swiglu_kernel(a_ref, wg_ref, wu_ref, o_ref):
    a = a_ref[...]
    g = jnp.dot(a, wg_ref[0].astype(BF16), preferred_element_type=F32)
    u = jnp.dot(a, wu_ref[0].astype(BF16), preferred_element_type=F32)
    o_ref[...] = (g * jax.nn.sigmoid(g) * u).astype(o_ref.dtype)


def _matmul_swiglu(a, wgu, layer, *, bm=1024, bn=256):
    m, k = a.shape
    f = wgu.shape[2] // 2
    nj = f // bn
    return pl.pallas_call(
        _mm_swiglu_kernel,
        out_shape=jax.ShapeDtypeStruct((m, f), BF16),
        grid=(m // bm, nj),
        in_specs=[pl.BlockSpec((bm, k), lambda i, j: (i, 0)),
                  pl.BlockSpec((1, k, bn), lambda i, j: (layer, 0, j)),
                  pl.BlockSpec((1, k, bn), lambda i, j: (layer, 0, j + nj))],
        out_specs=pl.BlockSpec((bm, bn), lambda i, j: (i, j)),
        compiler_params=_cparams(2),
        name="matmul_swiglu",
    )(a, wgu, wgu)


def _cast_kernel(w_ref, o_ref):
    o_ref[...] = w_ref[...].astype(o_ref.dtype)


def _to_bf16(w, *, rows=512):
    nl, k, n = w.shape
    return pl.pallas_call(
        _cast_kernel,
        out_shape=jax.ShapeDtypeStruct(w.shape, BF16),
        grid=(nl, k // rows),
        in_specs=[pl.BlockSpec((1, rows, n), lambda l, i: (l, i, 0))],
        out_specs=pl.BlockSpec((1, rows, n), lambda l, i: (l, i, 0)),
        compiler_params=_cparams(2),
        name="to_bf16",
    )(w)


def _mm_res_kernel(a_ref, w_ref, x_ref, gate_ref, o_ref, *, coef):
    y = jnp.dot(a_ref[...], w_ref[0].astype(BF16), preferred_element_type=F32)
    o_ref[...] = x_ref[...] + (coef * gate_ref[0]) * y


def _matmul_residual(a, w, layer, x2d, gate, seq, *, coef, bm=512, bn=512):
    m, k = a.shape
    n = w.shape[2]
    nb = gate.shape[0]
    per_b = seq // bm
    return pl.pallas_call(
        functools.partial(_mm_res_kernel, coef=coef),
        out_shape=jax.ShapeDtypeStruct((m, n), F32),
        grid=(m // bm, n // bn),
        in_specs=[pl.BlockSpec((bm, k), lambda i, j: (i, 0)),
                  pl.BlockSpec((1, k, bn), lambda i, j: (layer, 0, j)),
                  pl.BlockSpec((bm, bn), lambda i, j: (i, j)),
                  pl.BlockSpec((1, 1, bn), lambda i, j: (i // per_b, 0, j))],
        out_specs=pl.BlockSpec((bm, bn), lambda i, j: (i, j)),
        compiler_params=_cparams(2),
        name="matmul_residual",
    )(a, w, x2d, gate.reshape(nb, 1, n))


def _mm_res2_kernel(a1_ref, a2_ref, w1_ref, w2_ref, x_ref, gate_ref, o_ref, *, coef):
    y = jnp.dot(a1_ref[...], w1_ref[0].astype(BF16), preferred_element_type=F32)
    y = y + jnp.dot(a2_ref[...], w2_ref[0].astype(BF16), preferred_element_type=F32)
    o_ref[...] = x_ref[...] + (coef * gate_ref[0]) * y


def _matmul_residual2(a1, a2, w, layer, x2d, gate, seq, *, coef, bm=1024, bn=256):
    m, k1 = a1.shape
    k2 = a2.shape[1]
    assert k1 == k2
    n = w.shape[2]
    nb = gate.shape[0]
    per_b = seq // bm
    return pl.pallas_call(
        functools.partial(_mm_res2_kernel, coef=coef),
        out_shape=jax.ShapeDtypeStruct((m, n), F32),
        grid=(m // bm, n // bn),
        in_specs=[pl.BlockSpec((bm, k1), lambda i, j: (i, 0)),
                  pl.BlockSpec((bm, k2), lambda i, j: (i, 0)),
                  pl.BlockSpec((1, k1, bn), lambda i, j: (layer, 0, j)),
                  pl.BlockSpec((1, k2, bn), lambda i, j: (layer, 1, j)),
                  pl.BlockSpec((bm, bn), lambda i, j: (i, j)),
                  pl.BlockSpec((1, 1, bn), lambda i, j: (i // per_b, 0, j))],
        out_specs=pl.BlockSpec((bm, bn), lambda i, j: (i, j)),
        compiler_params=_cparams(2),
        name="matmul_residual2",
    )(a1, a2, w, w, x2d, gate.reshape(nb, 1, n))


def _conv_kernel(a_ref, g_ref, ap_ref, gp_ref, cw_ref, cb_ref, lng_ref, lnb_ref, on_ref, o_ref,
                 u_scr, y_scr):
    i = pl.program_id(1)
    tt, c = a_ref.shape[1], a_ref.shape[2]
    u_prev = ap_ref[0] * jax.nn.sigmoid(gp_ref[0])
    u_scr[0:CONV_HALO, :] = jnp.where(i > 0, u_prev, 0.0)
    u_scr[CONV_HALO:CONV_HALO + tt, :] = a_ref[0] * jax.nn.sigmoid(g_ref[0])
    base = CONV_HALO - (CONV_KERNEL - 1)
    taps = {}
    for j in range(CONV_KERNEL):
        taps.setdefault((base + j) % SUBLANE, []).append(j)
    for c0 in range(0, c, LANE):
        cols = slice(c0, c0 + LANE)
        for t0 in range(0, tt, CONV_ROWS):
            acc = jnp.broadcast_to(cb_ref[0, :, cols], (CONV_ROWS, LANE))
            for r, js in taps.items():
                span = (base + max(js)) // SUBLANE * SUBLANE + CONV_ROWS
                if r:
                    win = u_scr[t0:t0 + span + SUBLANE, cols]
                    win = pltpu.roll(win, span + SUBLANE - r, axis=0)
                else:
                    win = u_scr[t0:t0 + span, cols]
                for j in js:
                    q = (base + j) // SUBLANE * SUBLANE
                    acc = acc + cw_ref[j:j + 1, cols] * win[q:q + CONV_ROWS]
            y_scr[t0:t0 + CONV_ROWS, cols] = acc
    y = y_scr[...]
    mu = jnp.mean(y, axis=-1, keepdims=True)
    var = jnp.mean(jnp.square(y - mu), axis=-1, keepdims=True)
    z = (y - mu) * lax.rsqrt(var + EPS) * lng_ref[0] + lnb_ref[0]
    s = z * jax.nn.sigmoid(z)
    o_ref[0] = (s * lax.rsqrt(jnp.mean(s * s, axis=-1, keepdims=True) + EPS) * on_ref[0]).astype(o_ref.dtype)


def _conformer_conv(ag, layer, conv_w, conv_b, ln_g, ln_b, out_norm, *, tt=256):
    nb, seq, c2 = ag.shape
    c = c2 // 2
    hb = tt // CONV_HALO
    vec = lambda v: v.reshape(-1, 1, c)
    vec_spec = pl.BlockSpec((1, 1, c), lambda b, i: (layer, 0, 0))
    return pl.pallas_call(
        _conv_kernel,
        out_shape=jax.ShapeDtypeStruct((nb, seq, c), BF16),
        grid=(nb, seq // tt),
        in_specs=[
            pl.BlockSpec((1, tt, c), lambda b, i: (b, i, 0)),
            pl.BlockSpec((1, tt, c), lambda b, i: (b, i, 1)),
            pl.BlockSpec((1, CONV_HALO, c), lambda b, i: (b, jnp.maximum(i * hb - 1, 0), 0)),
            pl.BlockSpec((1, CONV_HALO, c), lambda b, i: (b, jnp.maximum(i * hb - 1, 0), 1)),
            pl.BlockSpec((None, CONV_KERNEL, c), lambda b, i: (layer, 0, 0)),
            vec_spec, vec_spec, vec_spec, vec_spec,
        ],
        out_specs=pl.BlockSpec((1, tt, c), lambda b, i: (b, i, 0)),
        scratch_shapes=[pltpu.VMEM((CONV_HALO + tt, c), F32), pltpu.VMEM((tt, c), F32)],
        compiler_params=_cparams(2),
        name="conformer_conv",
    )(ag, ag, ag, ag, conv_w, vec(conv_b), vec(ln_g), vec(ln_b), vec(out_norm))


def _attn_kernel(qit_ref, w_ref, ki_ref, q_ref, kt_ref, v_ref, on_ref, o_ref,
                 key_scr, bias_scr, y_scr, s_scr, p_scr, m_scr, a_scr, acc_scr, *, topk, idx_scale, attn_scale):
    i = pl.program_id(1)
    n_att = lax.shift_right_logical(i * Q_BLOCK, int(math.log2(ATT_CHUNK))) + 1
    halves = ATT_CHUNK // KEY_CHUNK
    n_key = n_att * halves
    n_heads = q_ref.shape[2] // HEAD_DIM
    group = n_heads // N_KV_HEADS
    i32 = jnp.int32
    q_pos = i * Q_BLOCK + lax.broadcasted_iota(i32, (KEY_CHUNK, Q_BLOCK), 1)
    row_iota = lax.broadcasted_iota(i32, (KEY_CHUNK, Q_BLOCK), 0)
    att_rows = lax.broadcasted_iota(i32, (ATT_CHUNK, Q_BLOCK), 0)

    def fold(x):
        return x.reshape(ATT_CHUNK // SUBLANE, SUBLANE, Q_BLOCK).sum(axis=0)

    def score_body(kc, kmax):
        r0 = pl.multiple_of(kc * KEY_CHUNK, KEY_CHUNK)
        ki_c = ki_ref[0, pl.ds(r0, KEY_CHUNK), :]
        acc = jnp.zeros((KEY_CHUNK, Q_BLOCK), F32)
        for hp in range(IDX_HEADS // 2):
            cols = slice(hp * 2 * Q_BLOCK, (hp + 1) * 2 * Q_BLOCK)
            s = jnp.dot(ki_c, qit_ref[0, 0, :, cols], preferred_element_type=F32)
            r = jnp.maximum(s, 0.0) * w_ref[0, 0, :, cols]
            acc = acc + (r[:, :Q_BLOCK] + r[:, Q_BLOCK:])
        sc = acc * idx_scale
        sc = jnp.where(sc == 0.0, 0.0, sc)
        bits = lax.bitcast_convert_type(sc, i32)
        key = jnp.where(bits < 0, bits ^ i32(INT_MAX), bits)
        key = jnp.where(r0 + row_iota <= q_pos, key, i32(INT_MIN))
        key_scr[pl.ds(r0, KEY_CHUNK), :] = key
        return jnp.maximum(kmax, key.reshape(KEY_CHUNK // SUBLANE, SUBLANE, Q_BLOCK).max(axis=0))

    kmax = lax.fori_loop(0, n_key, score_body, jnp.full((SUBLANE, Q_BLOCK), INT_MIN, i32))
    kmax = kmax.max(axis=0, keepdims=True)

    def count(pred):
        def body(c, acc):
            r0 = pl.multiple_of(c * ATT_CHUNK, ATT_CHUNK)
            return acc + fold(jnp.where(pred(key_scr[pl.ds(r0, ATT_CHUNK), :], r0), 1, 0))
        return lax.fori_loop(0, n_att, body, jnp.zeros((SUBLANE, Q_BLOCK), i32)).sum(axis=0, keepdims=True)

    def bis_cond(st):
        return jnp.logical_and(st[3] > 0, st[4] < MAX_BISECT)

    def bis_body(st):
        lo, hi, cnt_lo, _, it = st
        for _ in range(BISECT_UNROLL):
            mid = (lo >> 1) + (hi >> 1) + (lo & hi & 1)
            cnt = count(lambda k, r0, mid=mid: k >= mid)
            ge = cnt >= topk
            hi = jnp.where(cnt == topk, mid + 1, jnp.where(ge, hi, mid))
            lo = jnp.where(ge, mid, lo)
            cnt_lo = jnp.where(ge, cnt, cnt_lo)
        active = jnp.max(jnp.where(hi - 1 > lo, 1, 0))
        return lo, hi, cnt_lo, active, it + BISECT_UNROLL

    lo0 = jnp.full((1, Q_BLOCK), INT_MIN, i32)
    hi0 = jnp.where(kmax == INT_MAX, kmax, kmax + 1)
    cnt0 = jnp.full((1, Q_BLOCK), 1, i32) * (n_att * ATT_CHUNK)
    thr, _, cnt_thr, _, _ = lax.while_loop(bis_cond, bis_body, (lo0, hi0, cnt0, i32(1), i32(0)))

    excess = jnp.where(thr > INT_MIN, cnt_thr - topk, 0)

    @pl.when(jnp.max(excess) > 0)
    def _():
        need = topk - count(lambda k, r0: k > thr)

        def pos_body(_, st):
            lo_p, hi_p = st
            mid = (lo_p + hi_p) >> 1
            ok = count(lambda k, r0: jnp.logical_and(k == thr, r0 + att_rows <= mid)) >= need
            return jnp.where(ok, lo_p, mid), jnp.where(ok, mid, hi_p)

        n_rows = n_att * ATT_CHUNK
        _, cut = lax.fori_loop(0, int(math.log2(key_scr.shape[0])) + 1, pos_body,
                               (jnp.full((1, Q_BLOCK), -1, i32), jnp.full((1, Q_BLOCK), 1, i32) * (n_rows - 1)))
        cut = jnp.where(excess > 0, cut, INT_MAX)

        def demote_body(c, carry):
            r0 = pl.multiple_of(c * ATT_CHUNK, ATT_CHUNK)
            k = key_scr[pl.ds(r0, ATT_CHUNK), :]
            drop = jnp.logical_and(k == thr, r0 + att_rows > cut)
            key_scr[pl.ds(r0, ATT_CHUNK), :] = jnp.where(drop, k - 1, k)
            return carry

        lax.fori_loop(0, n_att, demote_body, 0)

    thr = jnp.maximum(thr, i32(INT_MIN + 1))

    def bias_body(kc, carry):
        r0 = pl.multiple_of(kc * KEY_CHUNK, KEY_CHUNK)
        b_t = jnp.where(key_scr[pl.ds(r0, KEY_CHUNK), :] >= thr, 0.0, NEG_BIAS)
        bias_scr[kc] = b_t.T
        return carry

    lax.fori_loop(0, n_key, bias_body, 0)

    c_exp = attn_scale * math.log2(math.e)
    rows = group * Q_BLOCK
    m_scr[...] = jnp.full(m_scr.shape, M_INIT, F32)
    acc_scr[...] = jnp.zeros(acc_scr.shape, F32)

    def att_body(c, carry):
        r0 = pl.multiple_of(c * ATT_CHUNK, ATT_CHUNK)
        bias = jnp.concatenate([bias_scr[c * halves + h] for h in range(halves)], axis=1)
        for g in range(N_KV_HEADS):
            qg = jnp.concatenate(
                [q_ref[0, :, (g * group + j) * HEAD_DIM:(g * group + j + 1) * HEAD_DIM] for j in range(group)],
                axis=0)
            s = jnp.dot(qg, kt_ref[0, c, g * HEAD_DIM:(g + 1) * HEAD_DIM, :], preferred_element_type=F32)
            s_scr[g] = (s.reshape(group, Q_BLOCK, ATT_CHUNK) + bias[None]).reshape(rows, ATT_CHUNK)
        for g in range(N_KV_HEADS):
            for t0 in range(0, rows, ATT_STRIP):
                strip = slice(t0, t0 + ATT_STRIP)
                x = s_scr[g, strip, :]
                m_old = m_scr[g, strip, :]
                m_new = jnp.maximum(m_old, x.max(axis=1, keepdims=True))
                a_scr[g, strip, :] = jnp.exp2((m_old - m_new) * c_exp)
                m_scr[g, strip, :] = m_new
                p = jnp.exp2((x - jnp.tile(m_new, (1, ATT_CHUNK // LANE))) * c_exp)
                p_scr[g, strip, :] = p.astype(p_scr.dtype)
        for g in range(N_KV_HEADS):
            vv = v_ref[0, pl.ds(r0, ATT_CHUNK), g * HEAD_DIM:(g + 1) * HEAD_DIM]
            v_ones = jnp.concatenate([vv, jnp.ones_like(vv)], axis=1)
            pv = jnp.dot(p_scr[g], v_ones, preferred_element_type=F32)
            acc_scr[g] = jnp.tile(a_scr[g], (1, 2)) * acc_scr[g] + pv
        return carry

    lax.fori_loop(0, n_att, att_body, 0)
    for g in range(N_KV_HEADS):
        acc = acc_scr[g]
        o = acc[:, :HEAD_DIM] / acc[:, HEAD_DIM:]
        for j in range(group):
            y_scr[:, (g * group + j) * HEAD_DIM:(g * group + j + 1) * HEAD_DIM] = o[j * Q_BLOCK:(j + 1) * Q_BLOCK]

    y = y_scr[...]
    o_ref[0] = (y * lax.rsqrt(jnp.mean(y * y, axis=-1, keepdims=True) + EPS) * on_ref[0]).astype(o_ref.dtype)


def _dsa_attention(qkvqi, kw, out_norm, layer, attn_width):
    nb, seq, _ = qkvqi.shape
    kv_width = N_KV_HEADS * HEAD_DIM
    nblk = seq // Q_BLOCK
    topk = min(TOPK_MAX, seq // 4)
    idx_scale = (IDX_DIM ** -0.5) * (IDX_HEADS ** -0.5)
    attn_scale = HEAD_DIM ** -0.5
    k = qkvqi[:, :, attn_width:attn_width + kv_width]
    qi = qkvqi[:, :, attn_width + 2 * kv_width:]
    kt = k.reshape(nb, seq // ATT_CHUNK, ATT_CHUNK, kv_width).transpose(0, 1, 3, 2)
    qit = qi.reshape(nb, nblk, Q_BLOCK, IDX_HEADS, IDX_DIM).transpose(0, 1, 4, 3, 2)
    qit = qit.reshape(nb, nblk, IDX_DIM, IDX_HEADS * Q_BLOCK)
    ki = kw[:, :, :IDX_DIM].astype(BF16)
    wi = kw[:, :, IDX_DIM:IDX_DIM + IDX_HEADS]
    wrow = wi.reshape(nb, nblk, Q_BLOCK, IDX_HEADS).transpose(0, 1, 3, 2).reshape(nb, nblk, 1, IDX_HEADS * Q_BLOCK)
    v_blk = (attn_width + kv_width) // kv_width
    rows = attn_width // kv_width * Q_BLOCK
    return pl.pallas_call(
        functools.partial(_attn_kernel, topk=topk, idx_scale=idx_scale, attn_scale=attn_scale),
        out_shape=jax.ShapeDtypeStruct((nb, seq, attn_width), BF16),
        grid=(nb, nblk),
        in_specs=[
            pl.BlockSpec((1, 1, IDX_DIM, IDX_HEADS * Q_BLOCK), lambda b, i: (b, i, 0, 0)),
            pl.BlockSpec((1, 1, 1, IDX_HEADS * Q_BLOCK), lambda b, i: (b, i, 0, 0)),
            pl.BlockSpec((1, seq, IDX_DIM), lambda b, i: (b, 0, 0)),
            pl.BlockSpec((1, Q_BLOCK, attn_width), lambda b, i: (b, i, 0)),
            pl.BlockSpec((1, seq // ATT_CHUNK, kv_width, ATT_CHUNK), lambda b, i: (b, 0, 0, 0)),
            pl.BlockSpec((1, seq, kv_width), lambda b, i: (b, 0, v_blk)),
            pl.BlockSpec((1, 1, attn_width), lambda b, i: (layer, 0, 0)),
        ],
        out_specs=pl.BlockSpec((1, Q_BLOCK, attn_width), lambda b, i: (b, i, 0)),
        scratch_shapes=[
            pltpu.VMEM((seq, Q_BLOCK), jnp.int32),
            pltpu.VMEM((seq // KEY_CHUNK, Q_BLOCK, KEY_CHUNK), F32),
            pltpu.VMEM((Q_BLOCK, attn_width), F32),
            pltpu.VMEM((N_KV_HEADS, rows, ATT_CHUNK), F32),
            pltpu.VMEM((N_KV_HEADS, rows, ATT_CHUNK), BF16),
            pltpu.VMEM((N_KV_HEADS, rows, LANE), F32),
            pltpu.VMEM((N_KV_HEADS, rows, LANE), F32),
            pltpu.VMEM((N_KV_HEADS, rows, 2 * HEAD_DIM), F32),
        ],
        compiler_params=_cparams(2),
        name="dsa_attention",
    )(qit, wrow, ki, qkvqi, kt, qkvqi, out_norm.reshape(-1, 1, attn_width))


def _ffn(x2d, seq, layer, norm_g, shift, scale, gate, wgu, wd):
    h = _norm_mod(x2d, norm_g, layer, scale, shift, seq)
    act = _matmul_swiglu(h, wgu, layer)
    return _matmul_residual(act, wd, layer, x2d, gate, seq, coef=FFN_RES)


def kernel(x, c, ada_w, ada_b, ffn1_norm, ffn1_wgu, ffn1_wd, mix_norm, w_in, conv_w, conv_b, conv_ln_g,
           conv_ln_b, conv_out_norm, attn_out_norm, w_out, ffn2_norm, ffn2_wgu, ffn2_wd, final_norm):
    nb, seq, d = x.shape
    depth = ada_w.shape[0]
    conv_width = d // 2
    attn_width = d - conv_width
    kv_width = N_KV_HEADS * HEAD_DIM
    n_main = 2 * conv_width + attn_width + 2 * kv_width + IDX_HEADS * IDX_DIM
    n_tail = IDX_DIM + IDX_HEADS
    w_in_t = jnp.transpose(w_in, (0, 2, 1))
    w_tail_t = jnp.pad(w_in_t[:, n_main:n_main + n_tail, :], ((0, 0), (0, LANE - n_tail), (0, 0)))
    wd1, wd2 = _to_bf16(ffn1_wd), _to_bf16(ffn2_wd)

    mod = _modulation(c, ada_w, ada_b)
    x2d = x.reshape(nb * seq, d)
    for l in range(depth):
        sh1, sc1, g1, sh2, sc2, g2, sh3, sc3, g3 = (mod[l, :, i * d:(i + 1) * d] for i in range(N_MOD))
        x2d = _ffn(x2d, seq, l, ffn1_norm, sh1, sc1, g1, ffn1_wgu, wd1)

        h = _norm_mod(x2d, mix_norm, l, sc2, sh2, seq)
        ag = _matmul_nt(h, w_in_t, l, col0=0, ncols=2 * conv_width, out_dtype=F32, bm=1024, bn=256)
        qkvqi = _matmul_nt(h, w_in_t, l, col0=2 * conv_width, ncols=n_main - 2 * conv_width, out_dtype=BF16,
                           bm=1024, bn=256)
        kw = _matmul_nt(h, w_tail_t, l, col0=0, ncols=LANE, out_dtype=F32, bm=1024, bn=LANE)
        y_conv = _conformer_conv(ag.reshape(nb, seq, 2 * conv_width), l, conv_w, conv_b, conv_ln_g,
                                 conv_ln_b, conv_out_norm)
        y_attn = _dsa_attention(qkvqi.reshape(nb, seq, -1), kw.reshape(nb, seq, LANE), attn_out_norm, l,
                                attn_width)
        x2d = _matmul_residual2(y_conv.reshape(nb * seq, conv_width), y_attn.reshape(nb * seq, attn_width),
                                w_out, l, x2d, g2, seq, coef=1.0)

        x2d = _ffn(x2d, seq, l, ffn2_norm, sh3, sc3, g3, ffn2_wgu, wd2)
    return _final_norm(x2d, final_norm).reshape(nb, seq, d)
```

```python
import functools
import math

import jax
import jax.numpy as jnp
from jax import lax
from jax.experimental import pallas as pl
from jax.experimental.pallas import tpu as pltpu

HEAD_DIM = 128
N_KV_HEADS = 4
IDX_HEADS = 32
IDX_DIM = 64
TOPK_MAX = 256
CONV_KERNEL = 31
Q_BLOCK = 128
N_MOD = 9
EPS = 1e-6
FFN_RES = 0.5

LANE = 128
SUBLANE = 8
KEY_CHUNK = 512
ATT_CHUNK = 512
ATT_STRIP = 32
CONV_HALO = 32
CONV_ROWS = 128
VMEM_LIMIT = 56 * 1024 * 1024

INT_MIN = -(2 ** 31)
INT_MAX = 2 ** 31 - 1
NEG_BIAS = -2e30
M_INIT = -1e30
MAX_BISECT = 40
BISECT_UNROLL = 8

F32 = jnp.float32
BF16 = jnp.bfloat16


def _cparams(n_axes):
    return pltpu.CompilerParams(dimension_semantics=("parallel",) * n_axes, vmem_limit_bytes=VMEM_LIMIT)


def _mod_kernel(cb_ref, w_ref, b_ref, o_ref, act_scr, *, k_chunk):
    nb, kdim, _ = cb_ref.shape
    tn = w_ref.shape[2]

    @pl.when(jnp.logical_and(pl.program_id(0) == 0, pl.program_id(1) == 0))
    def _():
        cb = cb_ref[...]
        act_scr[...] = cb * jax.nn.sigmoid(cb)

    def body(kc, accs):
        k0 = pl.multiple_of(kc * k_chunk, k_chunk)
        w = w_ref[0, pl.ds(k0, k_chunk), :]
        out = []
        for b in range(nb):
            p = w * jnp.tile(act_scr[b, pl.ds(k0, k_chunk), :], (1, tn // LANE))
            out.append(accs[b] + p.reshape(k_chunk // SUBLANE, SUBLANE, tn).sum(axis=0))
        return tuple(out)

    accs = lax.fori_loop(0, kdim // k_chunk, body,
                         tuple(jnp.zeros((SUBLANE, tn), F32) for _ in range(nb)))
    for b in range(nb):
        o_ref[0, b:b + 1, :] = accs[b].sum(axis=0, keepdims=True) + b_ref[0]


def _modulation(c, ada_w, ada_b, *, tn=1024, k_chunk=64):
    nl, kdim, n = ada_w.shape
    nb = c.shape[0]
    cb = jnp.broadcast_to(c[:, :, None], (nb, kdim, LANE))
    return pl.pallas_call(
        functools.partial(_mod_kernel, k_chunk=k_chunk),
        out_shape=jax.ShapeDtypeStruct((nl, nb, n), F32),
        grid=(nl, n // tn),
        in_specs=[
            pl.BlockSpec((nb, kdim, LANE), lambda l, j: (0, 0, 0)),
            pl.BlockSpec((1, kdim, tn), lambda l, j: (l, 0, j)),
            pl.BlockSpec((1, 1, tn), lambda l, j: (l, 0, j)),
        ],
        out_specs=pl.BlockSpec((1, nb, tn), lambda l, j: (l, 0, j)),
        scratch_shapes=[pltpu.VMEM((nb, kdim, LANE), F32)],
        compiler_params=pltpu.CompilerParams(dimension_semantics=("arbitrary", "arbitrary"),
                                             vmem_limit_bytes=VMEM_LIMIT),
        name="adaln_mod",
    )(cb, ada_w, ada_b.reshape(nl, 1, n))


def _norm_mod_kernel(x_ref, g_ref, sc_ref, sh_ref, o_ref):
    x = x_ref[...]
    y = x * lax.rsqrt(jnp.mean(x * x, axis=-1, keepdims=True) + EPS) * g_ref[0]
    o_ref[...] = (y * (1.0 + sc_ref[0]) + sh_ref[0]).astype(o_ref.dtype)


def _norm_mod(x2d, gains, layer, scale, shift, seq, *, tm=512):
    m, d = x2d.shape
    nb = scale.shape[0]
    per_b = seq // tm
    return pl.pallas_call(
        _norm_mod_kernel,
        out_shape=jax.ShapeDtypeStruct((m, d), BF16),
        grid=(m // tm,),
        in_specs=[
            pl.BlockSpec((tm, d), lambda i: (i, 0)),
            pl.BlockSpec((1, 1, d), lambda i: (layer, 0, 0)),
            pl.BlockSpec((1, 1, d), lambda i: (i // per_b, 0, 0)),
            pl.BlockSpec((1, 1, d), lambda i: (i // per_b, 0, 0)),
        ],
        out_specs=pl.BlockSpec((tm, d), lambda i: (i, 0)),
        compiler_params=_cparams(1),
        name="norm_mod",
    )(x2d, gains.reshape(-1, 1, d), scale.reshape(nb, 1, d), shift.reshape(nb, 1, d))


def _rmsnorm_kernel(x_ref, g_ref, o_ref):
    x = x_ref[...]
    o_ref[...] = x * lax.rsqrt(jnp.mean(x * x, axis=-1, keepdims=True) + EPS) * g_ref[...]


def _final_norm(x2d, gain, *, tm=512):
    m, d = x2d.shape
    return pl.pallas_call(
        _rmsnorm_kernel,
        out_shape=jax.ShapeDtypeStruct((m, d), F32),
        grid=(m // tm,),
        in_specs=[pl.BlockSpec((tm, d), lambda i: (i, 0)), pl.BlockSpec((1, d), lambda i: (0, 0))],
        out_specs=pl.BlockSpec((tm, d), lambda i: (i, 0)),
        compiler_params=_cparams(1),
        name="final_norm",
    )(x2d, gain.reshape(1, d))


def _mm_nt_kernel(a_ref, wt_ref, o_ref):
    y = lax.dot_general(a_ref[...], wt_ref[0].astype(BF16), (((1,), (1,)), ((), ())), preferred_element_type=F32)
    o_ref[...] = y.astype(o_ref.dtype)


def _matmul_nt(a, wt, layer, *, col0, ncols, out_dtype, bm, bn):
    m, k = a.shape
    off = col0 // bn
    return pl.pallas_call(
        _mm_nt_kernel,
        out_shape=jax.ShapeDtypeStruct((m, ncols), out_dtype),
        grid=(m // bm, ncols // bn),
        in_specs=[pl.BlockSpec((bm, k), lambda i, j: (i, 0)),
                  pl.BlockSpec((1, bn, k), lambda i, j: (layer, j + off, 0))],
        out_specs=pl.BlockSpec((bm, bn), lambda i, j: (i, j)),
        compiler_params=_cparams(2),
        name="matmul_nt",
    )(a, wt)


def _mm_swiglu_kernel(a_ref, wg_ref, wu_ref, o_ref):
    a = a_ref[...]
    g = jnp.dot(a, wg_ref[0].astype(BF16), preferred_element_type=F32)
    u = jnp.dot(a, wu_ref[0].astype(BF16), preferred_element_type=F32)
    o_ref[...] = (g * jax.nn.sigmoid(g) * u).astype(o_ref.dtype)


def _matmul_swiglu(a, wgu, layer, *, bm=1024, bn=256, lhs_buffers=2):
    m, k = a.shape
    f = wgu.shape[2] // 2
    nj = f // bn
    a_spec = pl.BlockSpec((bm, k), lambda i, j: (i, 0))
    if lhs_buffers != 2:
        a_spec = pl.BlockSpec((bm, k), lambda i, j: (i, 0), pipeline_mode=pl.Buffered(lhs_buffers))
    return pl.pallas_call(
        _mm_swiglu_kernel,
        out_shape=jax.ShapeDtypeStruct((m, f), BF16),
        grid=(m // bm, nj),
        in_specs=[a_spec,
                  pl.BlockSpec((1, k, bn), lambda i, j: (layer, 0, j)),
                  pl.BlockSpec((1, k, bn), lambda i, j: (layer, 0, j + nj))],
        out_specs=pl.BlockSpec((bm, bn), lambda i, j: (i, j)),
        compiler_params=_cparams(2),
        name="matmul_swiglu",
    )(a, wgu, wgu)


def _mm_swiglu_ws_kernel(a_ref, wg_ref, wu_ref, o_ref, wb_scr):
    bn = wg_ref.shape[2]

    @pl.when(pl.program_id(1) == 0)
    def _():
        wb_scr[:, :bn] = wg_ref[0].astype(BF16)
        wb_scr[:, bn:] = wu_ref[0].astype(BF16)

    y = jnp.dot(a_ref[...], wb_scr[...], preferred_element_type=F32)
    g, u = y[:, :bn], y[:, bn:]
    o_ref[...] = (g * jax.nn.sigmoid(g) * u).astype(o_ref.dtype)


def _matmul_swiglu_ws(a, wgu, layer, *, bm=1024, bn=256):
    m, k = a.shape
    f = wgu.shape[2] // 2
    nj = f // bn
    return pl.pallas_call(
        _mm_swiglu_ws_kernel,
        out_shape=jax.ShapeDtypeStruct((m, f), BF16),
        grid=(nj, m // bm),
        in_specs=[pl.BlockSpec((bm, k), lambda j, i: (i, 0)),
                  pl.BlockSpec((1, k, bn), lambda j, i: (layer, 0, j)),
                  pl.BlockSpec((1, k, bn), lambda j, i: (layer, 0, j + nj))],
        out_specs=pl.BlockSpec((bm, bn), lambda j, i: (i, j)),
        scratch_shapes=[pltpu.VMEM((k, 2 * bn), BF16)],
        compiler_params=pltpu.CompilerParams(dimension_semantics=("arbitrary", "arbitrary"),
                                             vmem_limit_bytes=VMEM_LIMIT),
        name="matmul_swiglu_ws",
    )(a, wgu, wgu)


def _cast_kernel(w_ref, o_ref):
    o_ref[...] = w_ref[...].astype(o_ref.dtype)


def _to_bf16(w, *, rows=512):
    nl, k, n = w.shape
    return pl.pallas_call(
        _cast_kernel,
        out_shape=jax.ShapeDtypeStruct(w.shape, BF16),
        grid=(nl, k // rows),
        in_specs=[pl.BlockSpec((1, rows, n), lambda l, i: (l, i, 0))],
        out_specs=pl.BlockSpec((1, rows, n), lambda l, i: (l, i, 0)),
        compiler_params=_cparams(2),
        name="to_bf16",
    )(w)


def _mm_res_kernel(a_ref, w_ref, x_ref, gate_ref, o_ref, *, coef):
    y = jnp.dot(a_ref[...], w_ref[0].astype(BF16), preferred_element_type=F32)
    o_ref[...] = x_ref[...] + (coef * gate_ref[0]) * y


def _matmul_residual(a, w, layer, x2d, gate, seq, *, coef, bm=512, bn=512):
    m, k = a.shape
    n = w.shape[2]
    nb = gate.shape[0]
    per_b = seq // bm
    return pl.pallas_call(
        functools.partial(_mm_res_kernel, coef=coef),
        out_shape=jax.ShapeDtypeStruct((m, n), F32),
        grid=(m // bm, n // bn),
        in_specs=[pl.BlockSpec((bm, k), lambda i, j: (i, 0)),
                  pl.BlockSpec((1, k, bn), lambda i, j: (layer, 0, j)),
                  pl.BlockSpec((bm, bn), lambda i, j: (i, j)),
                  pl.BlockSpec((1, 1, bn), lambda i, j: (i // per_b, 0, j))],
        out_specs=pl.BlockSpec((bm, bn), lambda i, j: (i, j)),
        compiler_params=_cparams(2),
        name="matmul_residual",
    )(a, w, x2d, gate.reshape(nb, 1, n))


def _mm_res2_kernel(a1_ref, a2_ref, w1_ref, w2_ref, x_ref, gate_ref, o_ref, *, coef):
    y = jnp.dot(a1_ref[...], w1_ref[0].astype(BF16), preferred_element_type=F32)
    y = y + jnp.dot(a2_ref[...], w2_ref[0].astype(BF16), preferred_element_type=F32)
    o_ref[...] = x_ref[...] + (coef * gate_ref[0]) * y


def _matmul_residual2(a1, a2, w, layer, x2d, gate, seq, *, coef, bm=1024, bn=512):
    m, k1 = a1.shape
    k2 = a2.shape[1]
    assert k1 == k2
    n = w.shape[2]
    nb = gate.shape[0]
    per_b = seq // bm
    return pl.pallas_call(
        functools.partial(_mm_res2_kernel, coef=coef),
        out_shape=jax.ShapeDtypeStruct((m, n), F32),
        grid=(m // bm, n // bn),
        in_specs=[pl.BlockSpec((bm, k1), lambda i, j: (i, 0)),
                  pl.BlockSpec((bm, k2), lambda i, j: (i, 0)),
                  pl.BlockSpec((1, k1, bn), lambda i, j: (layer, 0, j)),
                  pl.BlockSpec((1, k2, bn), lambda i, j: (layer, 1, j)),
                  pl.BlockSpec((bm, bn), lambda i, j: (i, j)),
                  pl.BlockSpec((1, 1, bn), lambda i, j: (i // per_b, 0, j))],
        out_specs=pl.BlockSpec((bm, bn), lambda i, j: (i, j)),
        compiler_params=_cparams(2),
        name="matmul_residual2",
    )(a1, a2, w, w, x2d, gate.reshape(nb, 1, n))


def _conv_kernel(a_ref, g_ref, ap_ref, gp_ref, cw_ref, cb_ref, lng_ref, lnb_ref, on_ref, o_ref,
                 u_scr, y_scr):
    i = pl.program_id(1)
    tt, c = a_ref.shape[1], a_ref.shape[2]
    u_prev = ap_ref[0] * jax.nn.sigmoid(gp_ref[0])
    u_scr[0:CONV_HALO, :] = jnp.where(i > 0, u_prev, 0.0)
    u_scr[CONV_HALO:CONV_HALO + tt, :] = a_ref[0] * jax.nn.sigmoid(g_ref[0])
    base = CONV_HALO - (CONV_KERNEL - 1)
    taps = {}
    for j in range(CONV_KERNEL):
        taps.setdefault((base + j) % SUBLANE, []).append(j)
    for c0 in range(0, c, LANE):
        cols = slice(c0, c0 + LANE)
        for t0 in range(0, tt, CONV_ROWS):
            acc = jnp.broadcast_to(cb_ref[0, :, cols], (CONV_ROWS, LANE))
            for r, js in taps.items():
                span = (base + max(js)) // SUBLANE * SUBLANE + CONV_ROWS
                if r:
                    win = u_scr[t0:t0 + span + SUBLANE, cols]
                    win = pltpu.roll(win, span + SUBLANE - r, axis=0)
                else:
                    win = u_scr[t0:t0 + span, cols]
                for j in js:
                    q = (base + j) // SUBLANE * SUBLANE
                    acc = acc + cw_ref[j:j + 1, cols] * win[q:q + CONV_ROWS]
            y_scr[t0:t0 + CONV_ROWS, cols] = acc
    y = y_scr[...]
    mu = jnp.mean(y, axis=-1, keepdims=True)
    var = jnp.mean(jnp.square(y - mu), axis=-1, keepdims=True)
    z = (y - mu) * lax.rsqrt(var + EPS) * lng_ref[0] + lnb_ref[0]
    s = z * jax.nn.sigmoid(z)
    o_ref[0] = (s * lax.rsqrt(jnp.mean(s * s, axis=-1, keepdims=True) + EPS) * on_ref[0]).astype(o_ref.dtype)


def _conformer_conv(ag, layer, conv_w, conv_b, ln_g, ln_b, out_norm, *, tt=256):
    nb, seq, c2 = ag.shape
    c = c2 // 2
    hb = tt // CONV_HALO
    vec = lambda v: v.reshape(-1, 1, c)
    vec_spec = pl.BlockSpec((1, 1, c), lambda b, i: (layer, 0, 0))
    return pl.pallas_call(
        _conv_kernel,
        out_shape=jax.ShapeDtypeStruct((nb, seq, c), BF16),
        grid=(nb, seq // tt),
        in_specs=[
            pl.BlockSpec((1, tt, c), lambda b, i: (b, i, 0)),
            pl.BlockSpec((1, tt, c), lambda b, i: (b, i, 1)),
            pl.BlockSpec((1, CONV_HALO, c), lambda b, i: (b, jnp.maximum(i * hb - 1, 0), 0)),
            pl.BlockSpec((1, CONV_HALO, c), lambda b, i: (b, jnp.maximum(i * hb - 1, 0), 1)),
            pl.BlockSpec((None, CONV_KERNEL, c), lambda b, i: (layer, 0, 0)),
            vec_spec, vec_spec, vec_spec, vec_spec,
        ],
        out_specs=pl.BlockSpec((1, tt, c), lambda b, i: (b, i, 0)),
        scratch_shapes=[pltpu.VMEM((CONV_HALO + tt, c), F32), pltpu.VMEM((tt, c), F32)],
        compiler_params=_cparams(2),
        name="conformer_conv",
    )(ag, ag, ag, ag, conv_w, vec(conv_b), vec(ln_g), vec(ln_b), vec(out_norm))


def _attn_kernel(qit_ref, w_ref, ki_ref, q_ref, k_ref, v_ref, on_ref, o_ref,
                 key_scr, bias_scr, y_scr, s_scr, p_scr, m_scr, a_scr, acc_scr, *, topk, idx_scale, attn_scale):
    i = pl.program_id(1)
    n_att = lax.shift_right_logical(i * Q_BLOCK, int(math.log2(ATT_CHUNK))) + 1
    halves = ATT_CHUNK // KEY_CHUNK
    n_key = n_att * halves
    n_heads = q_ref.shape[2] // HEAD_DIM
    group = n_heads // N_KV_HEADS
    i32 = jnp.int32
    q_pos = i * Q_BLOCK + lax.broadcasted_iota(i32, (KEY_CHUNK, Q_BLOCK), 1)
    row_iota = lax.broadcasted_iota(i32, (KEY_CHUNK, Q_BLOCK), 0)
    att_rows = lax.broadcasted_iota(i32, (ATT_CHUNK, Q_BLOCK), 0)

    def fold(x):
        return x.reshape(ATT_CHUNK // SUBLANE, SUBLANE, Q_BLOCK).sum(axis=0)

    def score_body(kc, kmax):
        r0 = pl.multiple_of(kc * KEY_CHUNK, KEY_CHUNK)
        ki_c = ki_ref[0, pl.ds(r0, KEY_CHUNK), :]
        acc = jnp.zeros((KEY_CHUNK, Q_BLOCK), F32)
        for hp in range(IDX_HEADS // 2):
            cols = slice(hp * 2 * Q_BLOCK, (hp + 1) * 2 * Q_BLOCK)
            s = jnp.dot(ki_c, qit_ref[0, 0, :, cols], preferred_element_type=F32)
            r = jnp.maximum(s, 0.0) * w_ref[0, 0, :, cols]
            acc = acc + (r[:, :Q_BLOCK] + r[:, Q_BLOCK:])
        sc = acc * idx_scale
        sc = jnp.where(sc == 0.0, 0.0, sc)
        bits = lax.bitcast_convert_type(sc, i32)
        key = jnp.where(bits < 0, bits ^ i32(INT_MAX), bits)
        key = jnp.where(r0 + row_iota <= q_pos, key, i32(INT_MIN))
        key_scr[pl.ds(r0, KEY_CHUNK), :] = key
        return jnp.maximum(kmax, key.reshape(KEY_CHUNK // SUBLANE, SUBLANE, Q_BLOCK).max(axis=0))

    kmax = lax.fori_loop(0, n_key, score_body, jnp.full((SUBLANE, Q_BLOCK), INT_MIN, i32))
    kmax = kmax.max(axis=0, keepdims=True)

    def count(pred):
        def body(c, acc):
            r0 = pl.multiple_of(c * ATT_CHUNK, ATT_CHUNK)
            return acc + fold(jnp.where(pred(key_scr[pl.ds(r0, ATT_CHUNK), :], r0), 1, 0))
        return lax.fori_loop(0, n_att, body, jnp.zeros((SUBLANE, Q_BLOCK), i32)).sum(axis=0, keepdims=True)

    def bis_cond(st):
        return jnp.logical_and(st[3] > 0, st[4] < MAX_BISECT)

    def bis_body(st):
        lo, hi, cnt_lo, _, it = st
        for _ in range(BISECT_UNROLL):
            mid = (lo >> 1) + (hi >> 1) + (lo & hi & 1)
            cnt = count(lambda k, r0, mid=mid: k >= mid)
            ge = cnt >= topk
            hi = jnp.where(cnt == topk, mid + 1, jnp.where(ge, hi, mid))
            lo = jnp.where(ge, mid, lo)
            cnt_lo = jnp.where(ge, cnt, cnt_lo)
        active = jnp.max(jnp.where(hi - 1 > lo, 1, 0))
        return lo, hi, cnt_lo, active, it + BISECT_UNROLL

    lo0 = jnp.full((1, Q_BLOCK), INT_MIN, i32)
    hi0 = jnp.where(kmax == INT_MAX, kmax, kmax + 1)
    cnt0 = jnp.full((1, Q_BLOCK), 1, i32) * (n_att * ATT_CHUNK)
    thr, _, cnt_thr, _, _ = lax.while_loop(bis_cond, bis_body, (lo0, hi0, cnt0, i32(1), i32(0)))

    excess = jnp.where(thr > INT_MIN, cnt_thr - topk, 0)

    @pl.when(jnp.max(excess) > 0)
    def _():
        need = topk - count(lambda k, r0: k > thr)

        def pos_body(_, st):
            lo_p, hi_p = st
            mid = (lo_p + hi_p) >> 1
            ok = count(lambda k, r0: jnp.logical_and(k == thr, r0 + att_rows <= mid)) >= need
            return jnp.where(ok, lo_p, mid), jnp.where(ok, mid, hi_p)

        n_rows = n_att * ATT_CHUNK
        _, cut = lax.fori_loop(0, int(math.log2(key_scr.shape[0])) + 1, pos_body,
                               (jnp.full((1, Q_BLOCK), -1, i32), jnp.full((1, Q_BLOCK), 1, i32) * (n_rows - 1)))
        cut = jnp.where(excess > 0, cut, INT_MAX)

        def demote_body(c, carry):
            r0 = pl.multiple_of(c * ATT_CHUNK, ATT_CHUNK)
            k = key_scr[pl.ds(r0, ATT_CHUNK), :]
            drop = jnp.logical_and(k == thr, r0 + att_rows > cut)
            key_scr[pl.ds(r0, ATT_CHUNK), :] = jnp.where(drop, k - 1, k)
            return carry

        lax.fori_loop(0, n_att, demote_body, 0)

    thr = jnp.maximum(thr, i32(INT_MIN + 1))

    def bias_body(kc, carry):
        r0 = pl.multiple_of(kc * KEY_CHUNK, KEY_CHUNK)
        b_t = jnp.where(key_scr[pl.ds(r0, KEY_CHUNK), :] >= thr, 0.0, NEG_BIAS)
        bias_scr[kc] = b_t.T
        return carry

    lax.fori_loop(0, n_key, bias_body, 0)

    c_exp = attn_scale * math.log2(math.e)
    rows = group * Q_BLOCK
    m_scr[...] = jnp.full(m_scr.shape, M_INIT, F32)
    acc_scr[...] = jnp.zeros(acc_scr.shape, F32)

    def att_body(c, carry):
        r0 = pl.multiple_of(c * ATT_CHUNK, ATT_CHUNK)
        bias = jnp.concatenate([bias_scr[c * halves + h] for h in range(halves)], axis=1)
        for g in range(N_KV_HEADS):
            qg = jnp.concatenate(
                [q_ref[0, :, (g * group + j) * HEAD_DIM:(g * group + j + 1) * HEAD_DIM] for j in range(group)],
                axis=0)
            kk = k_ref[0, pl.ds(r0, ATT_CHUNK), g * HEAD_DIM:(g + 1) * HEAD_DIM]
            s = lax.dot_general(qg, kk, (((1,), (1,)), ((), ())), preferred_element_type=F32)
            s_scr[g] = (s.reshape(group, Q_BLOCK, ATT_CHUNK) + bias[None]).reshape(rows, ATT_CHUNK)
        for g in range(N_KV_HEADS):
            for t0 in range(0, rows, ATT_STRIP):
                strip = slice(t0, t0 + ATT_STRIP)
                x = s_scr[g, strip, :]
                m_old = m_scr[g, strip, :]
                m_new = jnp.maximum(m_old, x.max(axis=1, keepdims=True))
                a_scr[g, strip, :] = jnp.exp2((m_old - m_new) * c_exp)
                m_scr[g, strip, :] = m_new
                p = jnp.exp2((x - jnp.tile(m_new, (1, ATT_CHUNK // LANE))) * c_exp)
                p_scr[g, strip, :] = p.astype(p_scr.dtype)
        for g in range(N_KV_HEADS):
            vv = v_ref[0, pl.ds(r0, ATT_CHUNK), g * HEAD_DIM:(g + 1) * HEAD_DIM]
            v_ones = jnp.concatenate([vv, jnp.ones_like(vv)], axis=1)
            pv = jnp.dot(p_scr[g], v_ones, preferred_element_type=F32)
            acc_scr[g] = jnp.tile(a_scr[g], (1, 2)) * acc_scr[g] + pv
        return carry

    lax.fori_loop(0, n_att, att_body, 0)
    for g in range(N_KV_HEADS):
        acc = acc_scr[g]
        o = acc[:, :HEAD_DIM] / acc[:, HEAD_DIM:]
        for j in range(group):
            y_scr[:, (g * group + j) * HEAD_DIM:(g * group + j + 1) * HEAD_DIM] = o[j * Q_BLOCK:(j + 1) * Q_BLOCK]

    y = y_scr[...]
    o_ref[0] = (y * lax.rsqrt(jnp.mean(y * y, axis=-1, keepdims=True) + EPS) * on_ref[0]).astype(o_ref.dtype)


def _dsa_attention(qkvqi, kw, out_norm, layer, attn_width):
    nb, seq, _ = qkvqi.shape
    kv_width = N_KV_HEADS * HEAD_DIM
    nblk = seq // Q_BLOCK
    topk = min(TOPK_MAX, seq // 4)
    idx_scale = (IDX_DIM ** -0.5) * (IDX_HEADS ** -0.5)
    attn_scale = HEAD_DIM ** -0.5
    qi = qkvqi[:, :, attn_width + 2 * kv_width:]
    qit =qi.reshape(nb, nblk, Q_BLOCK, IDX_HEADS, IDX_DIM).transpose(0, 1, 4, 3, 2)
    qit = qit.reshape(nb, nblk, IDX_DIM, IDX_HEADS * Q_BLOCK)
    ki = kw[:, :, :IDX_DIM].astype(BF16)
    wi = kw[:, :, IDX_DIM:IDX_DIM + IDX_HEADS]
    wrow = wi.reshape(nb, nblk, Q_BLOCK, IDX_HEADS).transpose(0, 1, 3, 2).reshape(nb, nblk, 1, IDX_HEADS * Q_BLOCK)
    v_blk = (attn_width + kv_width) // kv_width
    rows = attn_width // kv_width * Q_BLOCK
    return pl.pallas_call(
        functools.partial(_attn_kernel, topk=topk, idx_scale=idx_scale, attn_scale=attn_scale),
        out_shape=jax.ShapeDtypeStruct((nb, seq, attn_width), BF16),
        grid=(nb, nblk),
        in_specs=[
            pl.BlockSpec((1, 1, IDX_DIM, IDX_HEADS * Q_BLOCK), lambda b, i: (b, i, 0, 0)),
            pl.BlockSpec((1, 1, 1, IDX_HEADS * Q_BLOCK), lambda b, i: (b, i, 0, 0)),
            pl.BlockSpec((1, seq, IDX_DIM), lambda b, i: (b, 0, 0)),
            pl.BlockSpec((1, Q_BLOCK, attn_width), lambda b, i: (b, i, 0)),
            pl.BlockSpec((1, seq, kv_width), lambda b, i: (b, 0, v_blk - 1)),
            pl.BlockSpec((1, seq, kv_width), lambda b, i: (b, 0, v_blk)),
            pl.BlockSpec((1, 1, attn_width), lambda b, i: (layer, 0, 0)),
        ],
        out_specs=pl.BlockSpec((1, Q_BLOCK, attn_width), lambda b, i: (b, i, 0)),
        scratch_shapes=[
            pltpu.VMEM((seq, Q_BLOCK), jnp.int32),
            pltpu.VMEM((seq // KEY_CHUNK, Q_BLOCK, KEY_CHUNK), F32),
            pltpu.VMEM((Q_BLOCK, attn_width), F32),
            pltpu.VMEM((N_KV_HEADS, rows, ATT_CHUNK), F32),
            pltpu.VMEM((N_KV_HEADS, rows, ATT_CHUNK), BF16),
            pltpu.VMEM((N_KV_HEADS, rows, LANE), F32),
            pltpu.VMEM((N_KV_HEADS, rows, LANE), F32),
            pltpu.VMEM((N_KV_HEADS, rows, 2 * HEAD_DIM), F32),
        ],
        compiler_params=_cparams(2),
        name="dsa_attention",
    )(qit, wrow, ki, qkvqi, qkvqi, qkvqi, out_norm.reshape(-1, 1, attn_width))


def _ffn(x2d, seq, layer, norm_g, shift, scale, gate, wgu, wd, variant):
    h = _norm_mod(x2d, norm_g, layer, scale, shift, seq)
    if variant == "ws":
        act = _matmul_swiglu_ws(h, wgu, layer)
        return _matmul_residual(act, wd, layer, x2d, gate, seq, coef=FFN_RES, bm=1024, bn=256)
    if variant == "big_lhs":
        act = _matmul_swiglu(h, wgu, layer, bm=2048, lhs_buffers=1)
    else:
        act = _matmul_swiglu(h, wgu, layer)
    return _matmul_residual(act, wd, layer, x2d, gate, seq, coef=FFN_RES)


def kernel(x, c, ada_w, ada_b, ffn1_norm, ffn1_wgu, ffn1_wd, mix_norm, w_in, conv_w, conv_b, conv_ln_g,
           conv_ln_b, conv_out_norm, attn_out_norm, w_out, ffn2_norm, ffn2_wgu, ffn2_wd, final_norm):
    nb, seq, d = x.shape
    depth = ada_w.shape[0]
    conv_width = d // 2
    attn_width = d - conv_width
    kv_width = N_KV_HEADS * HEAD_DIM
    n_main = 2 * conv_width + attn_width + 2 * kv_width + IDX_HEADS * IDX_DIM
    n_tail = IDX_DIM + IDX_HEADS
    w_in_t = jnp.transpose(w_in, (0, 2, 1))
    w_tail_t = jnp.pad(w_in_t[:, n_main:n_main + n_tail, :], ((0, 0), (0, LANE - n_tail), (0, 0)))
    wd1, wd2 = _to_bf16(ffn1_wd), _to_bf16(ffn2_wd)

    mod = _modulation(c, ada_w, ada_b)
    x2d = x.reshape(nb * seq, d)
    for l in range(depth):
        sh1, sc1, g1, sh2, sc2, g2, sh3, sc3, g3 = (mod[l, :, i * d:(i + 1) * d] for i in range(N_MOD))
        x2d = _ffn(x2d, seq, l, ffn1_norm, sh1, sc1, g1, ffn1_wgu, wd1, ("base", "big_lhs")[l % 2])

        h = _norm_mod(x2d, mix_norm, l, sc2, sh2, seq)
        ag = _matmul_nt(h, w_in_t, l, col0=0, ncols=2 * conv_width, out_dtype=F32, bm=1024, bn=512)
        qkvqi = _matmul_nt(h, w_in_t, l, col0=2 * conv_width, ncols=n_main - 2 * conv_width, out_dtype=BF16,
                           bm=1024, bn=512)
        kw = _matmul_nt(h, w_tail_t, l, col0=0, ncols=LANE, out_dtype=F32, bm=1024, bn=LANE)
        y_conv = _conformer_conv(ag.reshape(nb, seq, 2 * conv_width), l, conv_w, conv_b, conv_ln_g,
                                 conv_ln_b, conv_out_norm)
        y_attn = _dsa_attention(qkvqi.reshape(nb, seq, -1), kw.reshape(nb, seq, LANE), attn_out_norm, l,
                                attn_width)
        x2d = _matmul_residual2(y_conv.reshape(nb * seq, conv_width), y_attn.reshape(nb * seq, attn_width),
                                w_out, l, x2d, g2, seq, coef=1.0)

        x2d = _ffn(x2d, seq, l, ffn2_norm, sh3, sc3, g3, ffn2_wgu, wd2, ("ws", "base")[l % 2])
    return _final_norm(x2d, final_norm).reshape(nb, seq, d)
```

```python
import functools
import math

import jax
import jax.numpy as jnp
from jax import lax
from jax.experimental import pallas as pl
from jax.experimental.pallas import tpu as pltpu

HEAD_DIM = 128
N_KV_HEADS = 4
IDX_HEADS = 32
IDX_DIM = 64
TOPK_MAX = 256
CONV_KERNEL = 31
Q_BLOCK = 128
N_MOD = 9
EPS = 1e-6
FFN_RES = 0.5

LANE = 128
SUBLANE = 8
KEY_CHUNK = 512
ATT_CHUNK = 512
ATT_STRIP = 32
CONV_HALO = 32
CONV_ROWS = 128
VMEM_LIMIT = 56 * 1024 * 1024

INT_MIN = -(2 ** 31)
INT_MAX = 2 ** 31 - 1
NEG_BIAS = -2e30
M_INIT = -1e30
MAX_BISECT = 40
BISECT_UNROLL = 8

F32 = jnp.float32
BF16 = jnp.bfloat16


def _cparams(n_axes):
    return pltpu.CompilerParams(dimension_semantics=("parallel",) * n_axes, vmem_limit_bytes=VMEM_LIMIT)


def _mod_kernel(cb_ref, w_ref, b_ref, o_ref, act_scr, *, k_chunk):
    nb, kdim, _ = cb_ref.shape
    tn = w_ref.shape[2]

    @pl.when(jnp.logical_and(pl.program_id(0) == 0, pl.program_id(1) == 0))
    def _():
        cb = cb_ref[...]
        act_scr[...] = cb * jax.nn.sigmoid(cb)

    def body(kc, accs):
        k0 = pl.multiple_of(kc * k_chunk, k_chunk)
        w = w_ref[0, pl.ds(k0, k_chunk), :]
        out = []
        for b in range(nb):
            p = w * jnp.tile(act_scr[b, pl.ds(k0, k_chunk), :], (1, tn // LANE))
            out.append(accs[b] + p.reshape(k_chunk // SUBLANE, SUBLANE, tn).sum(axis=0))
        return tuple(out)

    accs = lax.fori_loop(0, kdim // k_chunk, body,
                         tuple(jnp.zeros((SUBLANE, tn), F32) for _ in range(nb)))
    for b in range(nb):
        o_ref[0, b:b + 1, :] = accs[b].sum(axis=0, keepdims=True) + b_ref[0]


def _modulation(c, ada_w, ada_b, *, tn=1024, k_chunk=64):
    nl, kdim, n = ada_w.shape
    nb = c.shape[0]
    cb = jnp.broadcast_to(c[:, :, None], (nb, kdim, LANE))
    return pl.pallas_call(
        functools.partial(_mod_kernel, k_chunk=k_chunk),
        out_shape=jax.ShapeDtypeStruct((nl, nb, n), F32),
        grid=(nl, n // tn),
        in_specs=[
            pl.BlockSpec((nb, kdim, LANE), lambda l, j: (0, 0, 0)),
            pl.BlockSpec((1, kdim, tn), lambda l, j: (l, 0, j)),
            pl.BlockSpec((1, 1, tn), lambda l, j: (l, 0, j)),
        ],
        out_specs=pl.BlockSpec((1, nb, tn), lambda l, j: (l, 0, j)),
        scratch_shapes=[pltpu.VMEM((nb, kdim, LANE), F32)],
        compiler_params=pltpu.CompilerParams(dimension_semantics=("arbitrary", "arbitrary"),
                                             vmem_limit_bytes=VMEM_LIMIT),
        name="adaln_mod",
    )(cb, ada_w, ada_b.reshape(nl, 1, n))


def _norm_mod_kernel(x_ref, g_ref, sc_ref, sh_ref, o_ref):
    x = x_ref[...]
    y = x * lax.rsqrt(jnp.mean(x * x, axis=-1, keepdims=True) + EPS) * g_ref[0]
    o_ref[...] = (y * (1.0 + sc_ref[0]) + sh_ref[0]).astype(o_ref.dtype)


def _norm_mod(x2d, gains, layer, scale, shift, seq, *, tm=512):
    m, d = x2d.shape
    nb = scale.shape[0]
    per_b = seq // tm
    return pl.pallas_call(
        _norm_mod_kernel,
        out_shape=jax.ShapeDtypeStruct((m, d), BF16),
        grid=(m // tm,),
        in_specs=[
            pl.BlockSpec((tm, d), lambda i: (i, 0)),
            pl.BlockSpec((1, 1, d), lambda i: (layer, 0, 0)),
            pl.BlockSpec((1, 1, d), lambda i: (i // per_b, 0, 0)),
            pl.BlockSpec((1, 1, d), lambda i: (i // per_b, 0, 0)),
        ],
        out_specs=pl.BlockSpec((tm, d), lambda i: (i, 0)),
        compiler_params=_cparams(1),
        name="norm_mod",
    )(x2d, gains.reshape(-1, 1, d), scale.reshape(nb, 1, d), shift.reshape(nb, 1, d))


def _rmsnorm_kernel(x_ref, g_ref, o_ref):
    x = x_ref[...]
    o_ref[...] = x * lax.rsqrt(jnp.mean(x * x, axis=-1, keepdims=True) + EPS) * g_ref[...]


def _final_norm(x2d, gain, *, tm=512):
    m, d = x2d.shape
    return pl.pallas_call(
        _rmsnorm_kernel,
        out_shape=jax.ShapeDtypeStruct((m, d), F32),
        grid=(m // tm,),
        in_specs=[pl.BlockSpec((tm, d), lambda i: (i, 0)), pl.BlockSpec((1, d), lambda i: (0, 0))],
        out_specs=pl.BlockSpec((tm, d), lambda i: (i, 0)),
        compiler_params=_cparams(1),
        name="final_norm",
    )(x2d, gain.reshape(1, d))


def _mm_nt_kernel(a_ref, wt_ref, o_ref):
    y = lax.dot_general(a_ref[...], wt_ref[0].astype(BF16), (((1,), (1,)), ((), ())), preferred_element_type=F32)
    o_ref[...] = y.astype(o_ref.dtype)


def _matmul_nt(a, wt, layer, *, col0, ncols, out_dtype, bm, bn):
    m, k = a.shape
    off = col0 // bn
    return pl.pallas_call(
        _mm_nt_kernel,
        out_shape=jax.ShapeDtypeStruct((m, ncols), out_dtype),
        grid=(m // bm, ncols // bn),
        in_specs=[pl.BlockSpec((bm, k), lambda i, j: (i, 0)),
                  pl.BlockSpec((1, bn, k), lambda i, j: (layer, j + off, 0))],
        out_specs=pl.BlockSpec((bm, bn), lambda i, j: (i, j)),
        compiler_params=_cparams(2),
        name="matmul_nt",
    )(a, wt)


def _mm_swiglu_kernel(a_ref, wg_ref, wu_ref, o_ref):
    a = a_ref[...]
    g = jnp.dot(a, wg_ref[0].astype(BF16), preferred_element_type=F32)
    u = jnp.dot(a, wu_ref[0].astype(BF16), preferred_element_type=F32)
    o_ref[...] = (g * jax.nn.sigmoid(g) * u).astype(o_ref.dtype)


def _matmul_swiglu(a, wgu, layer, *, bm=2048, bn=256):
    m, k = a.shape
    f = wgu.shape[2] // 2
    nj = f // bn
    return pl.pallas_call(
        _mm_swiglu_kernel,
        out_shape=jax.ShapeDtypeStruct((m, f), BF16),
        grid=(m // bm, nj),
        in_specs=[pl.BlockSpec((bm, k), lambda i, j: (i, 0), pipeline_mode=pl.Buffered(1)),
                  pl.BlockSpec((1, k, bn), lambda i, j: (layer, 0, j)),
                  pl.BlockSpec((1, k, bn), lambda i, j: (layer, 0, j + nj))],
        out_specs=pl.BlockSpec((bm, bn), lambda i, j: (i, j)),
        compiler_params=_cparams(2),
        name="matmul_swiglu",
    )(a, wgu, wgu)


def _cast_kernel(w_ref, o_ref):
    o_ref[...] = w_ref[...].astype(o_ref.dtype)


def _to_bf16(w, *, rows=512):
    nl, k, n = w.shape
    return pl.pallas_call(
        _cast_kernel,
        out_shape=jax.ShapeDtypeStruct(w.shape, BF16),
        grid=(nl, k // rows),
        in_specs=[pl.BlockSpec((1, rows, n), lambda l, i: (l, i, 0))],
        out_specs=pl.BlockSpec((1, rows, n), lambda l, i: (l, i, 0)),
        compiler_params=_cparams(2),
        name="to_bf16",
    )(w)


def _mm_res_kernel(a_ref, w_ref, x_ref, gate_ref, o_ref, *, coef):
    y = jnp.dot(a_ref[...], w_ref[0].astype(BF16), preferred_element_type=F32)
    o_ref[...] = x_ref[...] + (coef * gate_ref[0]) * y


def _matmul_residual(a, w, layer, x2d, gate, seq, *, coef, bm=512, bn=512):
    m, k = a.shape
    n = w.shape[2]
    nb = gate.shape[0]
    per_b = seq // bm
    return pl.pallas_call(
        functools.partial(_mm_res_kernel, coef=coef),
        out_shape=jax.ShapeDtypeStruct((m, n), F32),
        grid=(m // bm, n // bn),
        in_specs=[pl.BlockSpec((bm, k), lambda i, j: (i, 0)),
                  pl.BlockSpec((1, k, bn), lambda i, j: (layer, 0, j)),
                  pl.BlockSpec((bm, bn), lambda i, j: (i, j)),
                  pl.BlockSpec((1, 1, bn), lambda i, j: (i // per_b, 0, j))],
        out_specs=pl.BlockSpec((bm, bn), lambda i, j: (i, j)),
        compiler_params=_cparams(2),
        name="matmul_residual",
    )(a, w, x2d, gate.reshape(nb, 1, n))


def _mm_res2_kernel(a1_ref, a2_ref, w1_ref, w2_ref, x_ref, gate_ref, o_ref, *, coef):
    y = jnp.dot(a1_ref[...], w1_ref[0].astype(BF16), preferred_element_type=F32)
    y = y + jnp.dot(a2_ref[...], w2_ref[0].astype(BF16), preferred_element_type=F32)
    o_ref[...] = x_ref[...] + (coef * gate_ref[0]) * y


def _matmul_residual2(a1, a2, w, layer, x2d, gate, seq, *, coef, bm=1024, bn=512):
    m, k1 = a1.shape
    k2 = a2.shape[1]
    assert k1 == k2
    n = w.shape[2]
    nb = gate.shape[0]
    per_b = seq // bm
    return pl.pallas_call(
        functools.partial(_mm_res2_kernel, coef=coef),
        out_shape=jax.ShapeDtypeStruct((m, n), F32),
        grid=(m // bm, n // bn),
        in_specs=[pl.BlockSpec((bm, k1), lambda i, j: (i, 0)),
                  pl.BlockSpec((bm, k2), lambda i, j: (i, 0)),
                  pl.BlockSpec((1, k1, bn), lambda i, j: (layer, 0, j)),
                  pl.BlockSpec((1, k2, bn), lambda i, j: (layer, 1, j)),
                  pl.BlockSpec((bm, bn), lambda i, j: (i, j)),
                  pl.BlockSpec((1, 1, bn), lambda i, j: (i // per_b, 0, j))],
        out_specs=pl.BlockSpec((bm, bn), lambda i, j: (i, j)),
        compiler_params=_cparams(2),
        name="matmul_residual2",
    )(a1, a2, w, w, x2d, gate.reshape(nb, 1, n))


def _conv_kernel(a_ref, g_ref, ap_ref, gp_ref, cw_ref, cb_ref, lng_ref, lnb_ref, on_ref, o_ref,
                 u_scr, y_scr):
    i = pl.program_id(1)
    tt, c = a_ref.shape[1], a_ref.shape[2]
    u_prev = ap_ref[0] * jax.nn.sigmoid(gp_ref[0])
    u_scr[0:CONV_HALO, :] = jnp.where(i > 0, u_prev, 0.0)
    u_scr[CONV_HALO:CONV_HALO + tt, :] = a_ref[0] * jax.nn.sigmoid(g_ref[0])
    base = CONV_HALO - (CONV_KERNEL - 1)
    taps = {}
    for j in range(CONV_KERNEL):
        taps.setdefault((base + j) % SUBLANE, []).append(j)
    for c0 in range(0, c, LANE):
        cols = slice(c0, c0 + LANE)
        for t0 in range(0, tt, CONV_ROWS):
            acc = jnp.broadcast_to(cb_ref[0, :, cols], (CONV_ROWS, LANE))
            for r, js in taps.items():
                span = (base + max(js)) // SUBLANE * SUBLANE + CONV_ROWS
                if r:
                    win = u_scr[t0:t0 + span + SUBLANE, cols]
                    win = pltpu.roll(win, span + SUBLANE - r, axis=0)
                else:
                    win = u_scr[t0:t0 + span, cols]
                for j in js:
                    q = (base + j) // SUBLANE * SUBLANE
                    acc = acc + cw_ref[j:j + 1, cols] * win[q:q + CONV_ROWS]
            y_scr[t0:t0 + CONV_ROWS, cols] = acc
    y = y_scr[...]
    mu = jnp.mean(y, axis=-1, keepdims=True)
    var = jnp.mean(jnp.square(y - mu), axis=-1, keepdims=True)
    z = (y - mu) * lax.rsqrt(var + EPS) * lng_ref[0] + lnb_ref[0]
    s = z * jax.nn.sigmoid(z)
    o_ref[0] = (s * lax.rsqrt(jnp.mean(s * s, axis=-1, keepdims=True) + EPS) * on_ref[0]).astype(o_ref.dtype)


def _conformer_conv(ag, layer, conv_w, conv_b, ln_g, ln_b, out_norm, *, tt=256):
    nb, seq, c2 = ag.shape
    c = c2 // 2
    hb = tt // CONV_HALO
    vec = lambda v: v.reshape(-1, 1, c)
    vec_spec = pl.BlockSpec((1, 1, c), lambda b, i: (layer, 0, 0))
    return pl.pallas_call(
        _conv_kernel,
        out_shape=jax.ShapeDtypeStruct((nb, seq, c), BF16),
        grid=(nb, seq // tt),
        in_specs=[
            pl.BlockSpec((1, tt, c), lambda b, i: (b, i, 0)),
            pl.BlockSpec((1, tt, c), lambda b, i: (b, i, 1)),
            pl.BlockSpec((1, CONV_HALO, c), lambda b, i: (b, jnp.maximum(i * hb - 1, 0), 0)),
            pl.BlockSpec((1, CONV_HALO, c), lambda b, i: (b, jnp.maximum(i * hb - 1, 0), 1)),
            pl.BlockSpec((None, CONV_KERNEL, c), lambda b, i: (layer, 0, 0)),
            vec_spec, vec_spec, vec_spec, vec_spec,
        ],
        out_specs=pl.BlockSpec((1, tt, c), lambda b, i: (b, i, 0)),
        scratch_shapes=[pltpu.VMEM((CONV_HALO + tt, c), F32), pltpu.VMEM((tt, c), F32)],
        compiler_params=_cparams(2),
        name="conformer_conv",
    )(ag, ag, ag, ag, conv_w, vec(conv_b), vec(ln_g), vec(ln_b), vec(out_norm))


def _attn_kernel(qia_ref, qib_ref, w_ref, ki_ref, q_ref, k_ref, v_ref, on_ref, o_ref,
                 key_scr, bias_scr, y_scr, s_scr, p_scr, m_scr, a_scr, acc_scr, qit_scr,
                 *, topk, idx_scale, attn_scale, att_strip):
    i = pl.program_id(1)
    n_att = lax.shift_right_logical(i * Q_BLOCK, int(math.log2(ATT_CHUNK))) + 1
    halves = ATT_CHUNK // KEY_CHUNK
    n_key = n_att * halves
    n_heads = q_ref.shape[2] // HEAD_DIM
    group = n_heads // N_KV_HEADS
    i32 = jnp.int32
    q_pos = i * Q_BLOCK + lax.broadcasted_iota(i32, (KEY_CHUNK, Q_BLOCK), 1)
    row_iota = lax.broadcasted_iota(i32, (KEY_CHUNK, Q_BLOCK), 0)
    att_rows = lax.broadcasted_iota(i32, (ATT_CHUNK, Q_BLOCK), 0)

    def fold(x):
        return x.reshape(ATT_CHUNK // SUBLANE, SUBLANE, Q_BLOCK).sum(axis=0)

    pair = 2 * IDX_DIM
    for half, ref in enumerate((qia_ref, qib_ref)):
        for hp in range(ref.shape[2] // pair):
            t = ref[0, :, hp * pair:(hp + 1) * pair].astype(F32).T
            c0 = (half * (ref.shape[2] // pair) + hp) * 2 * Q_BLOCK
            qit_scr[:, c0:c0 + Q_BLOCK] = t[:IDX_DIM].astype(qit_scr.dtype)
            qit_scr[:, c0 + Q_BLOCK:c0 + 2 * Q_BLOCK] = t[IDX_DIM:].astype(qit_scr.dtype)

    def score_body(kc, kmax):
        r0 = pl.multiple_of(kc * KEY_CHUNK, KEY_CHUNK)
        ki_c = ki_ref[0, pl.ds(r0, KEY_CHUNK), :]
        acc = jnp.zeros((KEY_CHUNK, Q_BLOCK), F32)
        for hp in range(IDX_HEADS // 2):
            cols = slice(hp * 2 * Q_BLOCK, (hp + 1) * 2 * Q_BLOCK)
            s = jnp.dot(ki_c, qit_scr[:, cols], preferred_element_type=F32)
            r = jnp.maximum(s, 0.0) * w_ref[0, 0, :, cols]
            acc = acc + (r[:, :Q_BLOCK] + r[:, Q_BLOCK:])
        sc = acc * idx_scale
        sc = jnp.where(sc == 0.0, 0.0, sc)
        bits = lax.bitcast_convert_type(sc, i32)
        key = jnp.where(bits < 0, bits ^ i32(INT_MAX), bits)
        key = jnp.where(r0 + row_iota <= q_pos, key, i32(INT_MIN))
        key_scr[pl.ds(r0, KEY_CHUNK), :] = key
        return jnp.maximum(kmax, key.reshape(KEY_CHUNK // SUBLANE, SUBLANE, Q_BLOCK).max(axis=0))

    kmax = lax.fori_loop(0, n_key, score_body, jnp.full((SUBLANE, Q_BLOCK), INT_MIN, i32))
    kmax = kmax.max(axis=0, keepdims=True)

    def count(pred):
        def body(c, acc):
            r0 = pl.multiple_of(c * ATT_CHUNK, ATT_CHUNK)
            return acc + fold(jnp.where(pred(key_scr[pl.ds(r0, ATT_CHUNK), :], r0), 1, 0))
        return lax.fori_loop(0, n_att, body, jnp.zeros((SUBLANE, Q_BLOCK), i32)).sum(axis=0, keepdims=True)

    def bis_cond(st):
        return jnp.logical_and(st[3] > 0, st[4] < MAX_BISECT)

    def bis_body(st):
        lo, hi, cnt_lo, _, it = st
        for _ in range(BISECT_UNROLL):
            mid = (lo >> 1) + (hi >> 1) + (lo & hi & 1)
            cnt = count(lambda k, r0, mid=mid: k >= mid)
            ge = cnt >= topk
            hi = jnp.where(cnt == topk, mid + 1, jnp.where(ge, hi, mid))
            lo = jnp.where(ge, mid, lo)
            cnt_lo = jnp.where(ge, cnt, cnt_lo)
        active = jnp.max(jnp.where(hi - 1 > lo, 1, 0))
        return lo, hi, cnt_lo, active, it + BISECT_UNROLL

    lo0 = jnp.full((1, Q_BLOCK), INT_MIN, i32)
    hi0 = jnp.where(kmax == INT_MAX, kmax, kmax + 1)
    cnt0 = jnp.full((1, Q_BLOCK), 1, i32) * (n_att * ATT_CHUNK)
    thr, _, cnt_thr, _, _ = lax.while_loop(bis_cond, bis_body, (lo0, hi0, cnt0, i32(1), i32(0)))

    excess = jnp.where(thr > INT_MIN, cnt_thr - topk, 0)

    @pl.when(jnp.max(excess) > 0)
    def _():
        need = topk - count(lambda k, r0: k > thr)

        def pos_body(_, st):
            lo_p, hi_p = st
            mid = (lo_p + hi_p) >> 1
            ok = count(lambda k, r0: jnp.logical_and(k == thr, r0 + att_rows <= mid)) >= need
            return jnp.where(ok, lo_p, mid), jnp.where(ok, mid, hi_p)

        n_rows = n_att * ATT_CHUNK
        _, cut = lax.fori_loop(0, int(math.log2(key_scr.shape[0])) + 1, pos_body,
                               (jnp.full((1, Q_BLOCK), -1, i32), jnp.full((1, Q_BLOCK), 1, i32) * (n_rows - 1)))
        cut = jnp.where(excess > 0, cut, INT_MAX)

        def demote_body(c, carry):
            r0 = pl.multiple_of(c * ATT_CHUNK, ATT_CHUNK)
            k = key_scr[pl.ds(r0, ATT_CHUNK), :]
            drop = jnp.logical_and(k == thr, r0 + att_rows > cut)
            key_scr[pl.ds(r0, ATT_CHUNK), :] = jnp.where(drop, k - 1, k)
            return carry

        lax.fori_loop(0, n_att, demote_body, 0)

    thr = jnp.maximum(thr, i32(INT_MIN + 1))

    def bias_body(kc, carry):
        r0 = pl.multiple_of(kc * KEY_CHUNK, KEY_CHUNK)
        b_t = jnp.where(key_scr[pl.ds(r0, KEY_CHUNK), :] >= thr, 0.0, NEG_BIAS)
        bias_scr[kc] = b_t.T
        return carry

    lax.fori_loop(0, n_key, bias_body, 0)

    c_exp = attn_scale * math.log2(math.e)
    rows = group * Q_BLOCK
    m_scr[...] = jnp.full(m_scr.shape, M_INIT, F32)
    acc_scr[...] = jnp.zeros(acc_scr.shape, F32)

    def att_body(c, carry):
        r0 = pl.multiple_of(c * ATT_CHUNK, ATT_CHUNK)
        bias = jnp.concatenate([bias_scr[c * halves + h] for h in range(halves)], axis=1)
        for g in range(N_KV_HEADS):
            qg = jnp.concatenate(
                [q_ref[0, :, (g * group + j) * HEAD_DIM:(g * group + j + 1) * HEAD_DIM] for j in range(group)],
                axis=0)
            kk = k_ref[0, pl.ds(r0, ATT_CHUNK), g * HEAD_DIM:(g + 1) * HEAD_DIM]
            s = lax.dot_general(qg, kk, (((1,), (1,)), ((), ())), preferred_element_type=F32)
            s_scr[g] = (s.reshape(group, Q_BLOCK, ATT_CHUNK) + bias[None]).reshape(rows, ATT_CHUNK)
        for g in range(N_KV_HEADS):
            for t0 in range(0, rows, att_strip):
                strip = slice(t0, t0 + att_strip)
                x = s_scr[g, strip, :]
                m_old = m_scr[g, strip, :]
                m_new = jnp.maximum(m_old, x.max(axis=1, keepdims=True))
                a_scr[g, strip, :] = jnp.exp2((m_old - m_new) * c_exp)
                m_scr[g, strip, :] = m_new
                p = jnp.exp2((x - jnp.tile(m_new, (1, ATT_CHUNK // LANE))) * c_exp)
                p_scr[g, strip, :] = p.astype(p_scr.dtype)
        for g in range(N_KV_HEADS):
            vv = v_ref[0, pl.ds(r0, ATT_CHUNK), g * HEAD_DIM:(g + 1) * HEAD_DIM]
            v_ones = jnp.concatenate([vv, jnp.ones_like(vv)], axis=1)
            pv = jnp.dot(p_scr[g], v_ones, preferred_element_type=F32)
            acc_scr[g] = jnp.tile(a_scr[g], (1, 2)) * acc_scr[g] + pv
        return carry

    lax.fori_loop(0, n_att, att_body, 0)
    for g in range(N_KV_HEADS):
        acc = acc_scr[g]
        o = acc[:, :HEAD_DIM] / acc[:, HEAD_DIM:]
        for j in range(group):
            y_scr[:, (g * group + j) * HEAD_DIM:(g * group + j + 1) * HEAD_DIM] = o[j * Q_BLOCK:(j + 1) * Q_BLOCK]

    y = y_scr[...]
    o_ref[0] = (y * lax.rsqrt(jnp.mean(y * y, axis=-1, keepdims=True) + EPS) * on_ref[0]).astype(o_ref.dtype)


def _dsa_attention(qkvqi, kw, out_norm, layer, attn_width, *, att_strip=ATT_STRIP):
    nb, seq, _ = qkvqi.shape
    kv_width = N_KV_HEADS * HEAD_DIM
    nblk = seq // Q_BLOCK
    topk = min(TOPK_MAX, seq // 4)
    idx_scale = (IDX_DIM ** -0.5) * (IDX_HEADS ** -0.5)
    attn_scale = HEAD_DIM ** -0.5
    qi_half = IDX_HEADS * IDX_DIM // 2
    qi_blk = (attn_width + 2 * kv_width) // qi_half
    ki = kw[:, :, :IDX_DIM].astype(BF16)
    wi = kw[:, :, IDX_DIM:IDX_DIM + IDX_HEADS]
    wrow = wi.reshape(nb, nblk, Q_BLOCK, IDX_HEADS).transpose(0, 1, 3, 2).reshape(nb, nblk, 1, IDX_HEADS * Q_BLOCK)
    v_blk = (attn_width + kv_width) // kv_width
    rows = attn_width // kv_width * Q_BLOCK
    return pl.pallas_call(
        functools.partial(_attn_kernel, topk=topk, idx_scale=idx_scale, attn_scale=attn_scale,
                          att_strip=att_strip),
        out_shape=jax.ShapeDtypeStruct((nb, seq, attn_width), BF16),
        grid=(nb, nblk),
        in_specs=[
            pl.BlockSpec((1, Q_BLOCK, qi_half), lambda b, i: (b, i, qi_blk)),
            pl.BlockSpec((1, Q_BLOCK, qi_half), lambda b, i: (b, i, qi_blk + 1)),
            pl.BlockSpec((1, 1, 1, IDX_HEADS * Q_BLOCK), lambda b, i: (b, i, 0, 0)),
            pl.BlockSpec((1, seq, IDX_DIM), lambda b, i: (b, 0, 0)),
            pl.BlockSpec((1, Q_BLOCK, attn_width), lambda b, i: (b, i, 0)),
            pl.BlockSpec((1, seq, kv_width), lambda b, i: (b, 0, v_blk - 1)),
            pl.BlockSpec((1, seq, kv_width), lambda b, i: (b, 0, v_blk)),
            pl.BlockSpec((1, 1, attn_width), lambda b, i: (layer, 0, 0)),
        ],
        out_specs=pl.BlockSpec((1, Q_BLOCK, attn_width), lambda b, i: (b, i, 0)),
        scratch_shapes=[
            pltpu.VMEM((seq, Q_BLOCK), jnp.int32),
            pltpu.VMEM((seq // KEY_CHUNK, Q_BLOCK, KEY_CHUNK), F32),
            pltpu.VMEM((Q_BLOCK, attn_width), F32),
            pltpu.VMEM((N_KV_HEADS, rows, ATT_CHUNK), F32),
            pltpu.VMEM((N_KV_HEADS, rows, ATT_CHUNK), BF16),
            pltpu.VMEM((N_KV_HEADS, rows, LANE), F32),
            pltpu.VMEM((N_KV_HEADS, rows, LANE), F32),
            pltpu.VMEM((N_KV_HEADS, rows, 2 * HEAD_DIM), F32),
            pltpu.VMEM((IDX_DIM, IDX_HEADS * Q_BLOCK), BF16),
        ],
        compiler_params=_cparams(2),
        name="dsa_attention",
    )(qkvqi, qkvqi, wrow, ki, qkvqi, qkvqi, qkvqi, out_norm.reshape(-1, 1, attn_width))


def _ffn(x2d, seq, layer, norm_g, shift, scale, gate, wgu, wd):
    h = _norm_mod(x2d, norm_g, layer, scale, shift, seq)
    act = _matmul_swiglu(h, wgu, layer)
    return _matmul_residual(act, wd, layer, x2d, gate, seq, coef=FFN_RES)


def kernel(x, c, ada_w, ada_b, ffn1_norm, ffn1_wgu, ffn1_wd, mix_norm, w_in, conv_w, conv_b, conv_ln_g,
           conv_ln_b, conv_out_norm, attn_out_norm, w_out, ffn2_norm, ffn2_wgu, ffn2_wd, final_norm):
    nb, seq, d = x.shape
    depth = ada_w.shape[0]
    conv_width = d // 2
    attn_width = d - conv_width
    kv_width = N_KV_HEADS * HEAD_DIM
    n_main = 2 * conv_width + attn_width + 2 * kv_width + IDX_HEADS * IDX_DIM
    n_tail = IDX_DIM + IDX_HEADS
    w_in_t = jnp.transpose(w_in, (0, 2, 1))
    w_tail_t = jnp.pad(w_in_t[:, n_main:n_main + n_tail, :], ((0, 0), (0, LANE - n_tail), (0, 0)))
    wd1, wd2 = _to_bf16(ffn1_wd), _to_bf16(ffn2_wd)

    mod = _modulation(c, ada_w, ada_b)
    x2d = x.reshape(nb * seq, d)
    for l in range(depth):
        sh1, sc1, g1, sh2, sc2, g2, sh3, sc3, g3 = (mod[l, :, i * d:(i + 1) * d] for i in range(N_MOD))
        x2d = _ffn(x2d, seq, l, ffn1_norm, sh1, sc1, g1, ffn1_wgu, wd1)

        h = _norm_mod(x2d, mix_norm, l, sc2, sh2, seq)
        ag = _matmul_nt(h, w_in_t, l, col0=0, ncols=2 * conv_width, out_dtype=F32, bm=1024, bn=512)
        qkvqi = _matmul_nt(h, w_in_t, l, col0=2 * conv_width, ncols=n_main - 2 * conv_width, out_dtype=BF16,
                           bm=1024, bn=512)
        kw = _matmul_nt(h, w_tail_t, l, col0=0, ncols=LANE, out_dtype=F32, bm=1024, bn=LANE)
        y_conv = _conformer_conv(ag.reshape(nb, seq, 2 * conv_width), l, conv_w, conv_b, conv_ln_g,
                                 conv_ln_b, conv_out_norm, tt=(256, 512)[l % 2])
        y_attn = _dsa_attention(qkvqi.reshape(nb, seq, -1), kw.reshape(nb, seq, LANE), attn_out_norm, l,
                                attn_width, att_strip=(ATT_STRIP, 2 * ATT_STRIP)[l % 2])
        x2d = _matmul_residual2(y_conv.reshape(nb * seq, conv_width), y_attn.reshape(nb * seq, attn_width),
                                w_out, l, x2d, g2, seq, coef=1.0)

        x2d = _ffn(x2d, seq, l, ffn2_norm, sh3, sc3, g3, ffn2_wgu, wd2)
    return _final_norm(x2d, final_norm).reshape(nb, seq, d)
```

```python
import functools
import math

import jax
import jax.numpy as jnp
from jax import lax
from jax.experimental import pallas as pl
from jax.experimental.pallas import tpu as pltpu

HEAD_DIM = 128
N_KV_HEADS = 4
IDX_HEADS = 32
IDX_DIM = 64
TOPK_MAX = 256
CONV_KERNEL = 31
Q_BLOCK = 128
N_MOD = 9
EPS = 1e-6
FFN_RES = 0.5

LANE = 128
SUBLANE = 8
KEY_CHUNK = 512
ATT_CHUNK = 512
ATT_STRIP = 32
CONV_HALO = 32
CONV_ROWS = 128
VMEM_LIMIT = 56 * 1024 * 1024

INT_MIN = -(2 ** 31)
INT_MAX = 2 ** 31 - 1
NEG_BIAS = -2e30
M_INIT = -1e30
MAX_BISECT = 40
BISECT_UNROLL = 8

F32 = jnp.float32
BF16 = jnp.bfloat16


def _cparams(n_axes):
    return pltpu.CompilerParams(dimension_semantics=("parallel",) * n_axes, vmem_limit_bytes=VMEM_LIMIT)


def _mod_kernel(cb_ref, w_ref, b_ref, o_ref, act_scr, *, k_chunk):
    nb, kdim, _ = cb_ref.shape
    tn = w_ref.shape[2]

    @pl.when(jnp.logical_and(pl.program_id(0) == 0, pl.program_id(1) == 0))
    def _():
        cb = cb_ref[...]
        act_scr[...] = cb * jax.nn.sigmoid(cb)

    def body(kc, accs):
        k0 = pl.multiple_of(kc * k_chunk, k_chunk)
        w = w_ref[0, pl.ds(k0, k_chunk), :]
        out = []
        for b in range(nb):
            p = w * jnp.tile(act_scr[b, pl.ds(k0, k_chunk), :], (1, tn // LANE))
            out.append(accs[b] + p.reshape(k_chunk // SUBLANE, SUBLANE, tn).sum(axis=0))
        return tuple(out)

    accs = lax.fori_loop(0, kdim // k_chunk, body,
                         tuple(jnp.zeros((SUBLANE, tn), F32) for _ in range(nb)))
    for b in range(nb):
        o_ref[0, b:b + 1, :] = accs[b].sum(axis=0, keepdims=True) + b_ref[0]


def _modulation(c, ada_w, ada_b, *, tn=1024, k_chunk=64):
    nl, kdim, n = ada_w.shape
    nb = c.shape[0]
    cb = jnp.broadcast_to(c[:, :, None], (nb, kdim, LANE))
    return pl.pallas_call(
        functools.partial(_mod_kernel, k_chunk=k_chunk),
        out_shape=jax.ShapeDtypeStruct((nl, nb, n), F32),
        grid=(nl, n // tn),
        in_specs=[
            pl.BlockSpec((nb, kdim, LANE), lambda l, j: (0, 0, 0)),
            pl.BlockSpec((1, kdim, tn), lambda l, j: (l, 0, j)),
            pl.BlockSpec((1, 1, tn), lambda l, j: (l, 0, j)),
        ],
        out_specs=pl.BlockSpec((1, nb, tn), lambda l, j: (l, 0, j)),
        scratch_shapes=[pltpu.VMEM((nb, kdim, LANE), F32)],
        compiler_params=pltpu.CompilerParams(dimension_semantics=("arbitrary", "arbitrary"),
                                             vmem_limit_bytes=VMEM_LIMIT),
        name="adaln_mod",
    )(cb, ada_w, ada_b.reshape(nl, 1, n))


def _norm_mod_kernel(x_ref, g_ref, sc_ref, sh_ref, o_ref):
    x = x_ref[...]
    y = x * lax.rsqrt(jnp.mean(x * x, axis=-1, keepdims=True) + EPS) * g_ref[0]
    o_ref[...] = (y * (1.0 + sc_ref[0]) + sh_ref[0]).astype(o_ref.dtype)


def _norm_mod(x2d, gains, layer, scale, shift, seq, *, tm=512):
    m, d = x2d.shape
    nb = scale.shape[0]
    per_b = seq // tm
    return pl.pallas_call(
        _norm_mod_kernel,
        out_shape=jax.ShapeDtypeStruct((m, d), BF16),
        grid=(m // tm,),
        in_specs=[
            pl.BlockSpec((tm, d), lambda i: (i, 0)),
            pl.BlockSpec((1, 1, d), lambda i: (layer, 0, 0)),
            pl.BlockSpec((1, 1, d), lambda i: (i // per_b, 0, 0)),
            pl.BlockSpec((1, 1, d), lambda i: (i // per_b, 0, 0)),
        ],
        out_specs=pl.BlockSpec((tm, d), lambda i: (i, 0)),
        compiler_params=_cparams(1),
        name="norm_mod",
    )(x2d, gains.reshape(-1, 1, d), scale.reshape(nb, 1, d), shift.reshape(nb, 1, d))


def _rmsnorm_kernel(x_ref, g_ref, o_ref):
    x = x_ref[...]
    o_ref[...] = x * lax.rsqrt(jnp.mean(x * x, axis=-1, keepdims=True) + EPS) * g_ref[...]


def _final_norm(x2d, gain, *, tm=512):
    m, d = x2d.shape
    return pl.pallas_call(
        _rmsnorm_kernel,
        out_shape=jax.ShapeDtypeStruct((m, d), F32),
        grid=(m // tm,),
        in_specs=[pl.BlockSpec((tm, d), lambda i: (i, 0)), pl.BlockSpec((1, d), lambda i: (0, 0))],
        out_specs=pl.BlockSpec((tm, d), lambda i: (i, 0)),
        compiler_params=_cparams(1),
        name="final_norm",
    )(x2d, gain.reshape(1, d))


def _mm_nt_kernel(a_ref, wt_ref, o_ref):
    y = lax.dot_general(a_ref[...], wt_ref[0].astype(BF16), (((1,), (1,)), ((), ())), preferred_element_type=F32)
    o_ref[...] = y.astype(o_ref.dtype)


def _matmul_nt(a, wt, layer, *, col0, ncols, out_dtype, bm, bn):
    m, k = a.shape
    off = col0 // bn
    return pl.pallas_call(
        _mm_nt_kernel,
        out_shape=jax.ShapeDtypeStruct((m, ncols), out_dtype),
        grid=(m // bm, ncols // bn),
        in_specs=[pl.BlockSpec((bm, k), lambda i, j: (i, 0)),
                  pl.BlockSpec((1, bn, k), lambda i, j: (layer, j + off, 0))],
        out_specs=pl.BlockSpec((bm, bn), lambda i, j: (i, j)),
        compiler_params=_cparams(2),
        name="matmul_nt",
    )(a, wt)


def _mm_swiglu_kernel(a_ref, wg_ref, wu_ref, o_ref):
    a = a_ref[...]
    g = jnp.dot(a, wg_ref[0].astype(BF16), preferred_element_type=F32)
    u = jnp.dot(a, wu_ref[0].astype(BF16), preferred_element_type=F32)
    o_ref[...] = (g * jax.nn.sigmoid(g) * u).astype(o_ref.dtype)


def _matmul_swiglu(a, wgu, layer, *, bm=2048, bn=256):
    m, k = a.shape
    f = wgu.shape[2] // 2
    nj = f // bn
    return pl.pallas_call(
        _mm_swiglu_kernel,
        out_shape=jax.ShapeDtypeStruct((m, f), BF16),
        grid=(m // bm, nj),
        in_specs=[pl.BlockSpec((bm, k), lambda i, j: (i, 0), pipeline_mode=pl.Buffered(1)),
                  pl.BlockSpec((1, k, bn), lambda i, j: (layer, 0, j)),
                  pl.BlockSpec((1, k, bn), lambda i, j: (layer, 0, j + nj))],
        out_specs=pl.BlockSpec((bm, bn), lambda i, j: (i, j)),
        compiler_params=_cparams(2),
        name="matmul_swiglu",
    )(a, wgu, wgu)


def _cast_kernel(w_ref, o_ref):
    o_ref[...] = w_ref[...].astype(o_ref.dtype)


def _to_bf16(w, *, rows=512):
    nl, k, n = w.shape
    return pl.pallas_call(
        _cast_kernel,
        out_shape=jax.ShapeDtypeStruct(w.shape, BF16),
        grid=(nl, k // rows),
        in_specs=[pl.BlockSpec((1, rows, n), lambda l, i: (l, i, 0))],
        out_specs=pl.BlockSpec((1, rows, n), lambda l, i: (l, i, 0)),
        compiler_params=_cparams(2),
        name="to_bf16",
    )(w)


def _mm_res_kernel(a_ref, w_ref, x_ref, gate_ref, o_ref, *, coef):
    y = jnp.dot(a_ref[...], w_ref[0].astype(BF16), preferred_element_type=F32)
    o_ref[...] = x_ref[...] + (coef * gate_ref[0]) * y


def _matmul_residual(a, w, layer, x2d, gate, seq, *, coef, bm=512, bn=512):
    m, k = a.shape
    n = w.shape[2]
    nb = gate.shape[0]
    per_b = seq // bm
    return pl.pallas_call(
        functools.partial(_mm_res_kernel, coef=coef),
        out_shape=jax.ShapeDtypeStruct((m, n), F32),
        grid=(m // bm, n // bn),
        in_specs=[pl.BlockSpec((bm, k), lambda i, j: (i, 0)),
                  pl.BlockSpec((1, k, bn), lambda i, j: (layer, 0, j)),
                  pl.BlockSpec((bm, bn), lambda i, j: (i, j)),
                  pl.BlockSpec((1, 1, bn), lambda i, j: (i // per_b, 0, j))],
        out_specs=pl.BlockSpec((bm, bn), lambda i, j: (i, j)),
        compiler_params=_cparams(2),
        name="matmul_residual",
    )(a, w, x2d, gate.reshape(nb, 1, n))


def _mm_res2_kernel(a1_ref, a2_ref, w1_ref, w2_ref, x_ref, gate_ref, o_ref, *, coef):
    y = jnp.dot(a1_ref[...], w1_ref[0].astype(BF16), preferred_element_type=F32)
    y = y + jnp.dot(a2_ref[...], w2_ref[0].astype(BF16), preferred_element_type=F32)
    o_ref[...] = x_ref[...] + (coef * gate_ref[0]) * y


def _matmul_residual2(a1, a2, w, layer, x2d, gate, seq, *, coef, bm=1024, bn=512):
    m, k1 = a1.shape
    k2 = a2.shape[1]
    assert k1 == k2
    n = w.shape[2]
    nb = gate.shape[0]
    per_b = seq // bm
    return pl.pallas_call(
        functools.partial(_mm_res2_kernel, coef=coef),
        out_shape=jax.ShapeDtypeStruct((m, n), F32),
        grid=(m // bm, n // bn),
        in_specs=[pl.BlockSpec((bm, k1), lambda i, j: (i, 0)),
                  pl.BlockSpec((bm, k2), lambda i, j: (i, 0)),
                  pl.BlockSpec((1, k1, bn), lambda i, j: (layer, 0, j)),
                  pl.BlockSpec((1, k2, bn), lambda i, j: (layer, 1, j)),
                  pl.BlockSpec((bm, bn), lambda i, j: (i, j)),
                  pl.BlockSpec((1, 1, bn), lambda i, j: (i // per_b, 0, j))],
        out_specs=pl.BlockSpec((bm, bn), lambda i, j: (i, j)),
        compiler_params=_cparams(2),
        name="matmul_residual2",
    )(a1, a2, w, w, x2d, gate.reshape(nb, 1, n))


def _conv_kernel(a_ref, g_ref, ap_ref, gp_ref, cw_ref, cb_ref, lng_ref, lnb_ref, on_ref, o_ref,
                 u_scr, y_scr):
    i = pl.program_id(1)
    tt, c = a_ref.shape[1], a_ref.shape[2]
    u_prev = ap_ref[0] * jax.nn.sigmoid(gp_ref[0])
    u_scr[0:CONV_HALO, :] = jnp.where(i > 0, u_prev, 0.0)
    u_scr[CONV_HALO:CONV_HALO + tt, :] = a_ref[0] * jax.nn.sigmoid(g_ref[0])
    base = CONV_HALO - (CONV_KERNEL - 1)
    taps = {}
    for j in range(CONV_KERNEL):
        taps.setdefault((base + j) % SUBLANE, []).append(j)
    for c0 in range(0, c, LANE):
        cols = slice(c0, c0 + LANE)
        for t0 in range(0, tt, CONV_ROWS):
            acc = jnp.broadcast_to(cb_ref[0, :, cols], (CONV_ROWS, LANE))
            for r, js in taps.items():
                span = (base + max(js)) // SUBLANE * SUBLANE + CONV_ROWS
                if r:
                    win = u_scr[t0:t0 + span + SUBLANE, cols]
                    win = pltpu.roll(win, span + SUBLANE - r, axis=0)
                else:
                    win = u_scr[t0:t0 + span, cols]
                for j in js:
                    q = (base + j) // SUBLANE * SUBLANE
                    acc = acc + cw_ref[j:j + 1, cols] * win[q:q + CONV_ROWS]
            y_scr[t0:t0 + CONV_ROWS, cols] = acc
    y = y_scr[...]
    mu = jnp.mean(y, axis=-1, keepdims=True)
    var = jnp.mean(jnp.square(y - mu), axis=-1, keepdims=True)
    z = (y - mu) * lax.rsqrt(var + EPS) * lng_ref[0] + lnb_ref[0]
    s = z * jax.nn.sigmoid(z)
    o_ref[0] = (s * lax.rsqrt(jnp.mean(s * s, axis=-1, keepdims=True) + EPS) * on_ref[0]).astype(o_ref.dtype)


def _conformer_conv(ag, layer, conv_w, conv_b, ln_g, ln_b, out_norm, *, tt=256):
    nb, seq, c2 = ag.shape
    c = c2 // 2
    hb = tt // CONV_HALO
    vec = lambda v: v.reshape(-1, 1, c)
    vec_spec = pl.BlockSpec((1, 1, c), lambda b, i: (layer, 0, 0))
    return pl.pallas_call(
        _conv_kernel,
        out_shape=jax.ShapeDtypeStruct((nb, seq, c), BF16),
        grid=(nb, seq // tt),
        in_specs=[
            pl.BlockSpec((1, tt, c), lambda b, i: (b, i, 0)),
            pl.BlockSpec((1, tt, c), lambda b, i: (b, i, 1)),
            pl.BlockSpec((1, CONV_HALO, c), lambda b, i: (b, jnp.maximum(i * hb - 1, 0), 0)),
            pl.BlockSpec((1, CONV_HALO, c), lambda b, i: (b, jnp.maximum(i * hb - 1, 0), 1)),
            pl.BlockSpec((None, CONV_KERNEL, c), lambda b, i: (layer, 0, 0)),
            vec_spec, vec_spec, vec_spec, vec_spec,
        ],
        out_specs=pl.BlockSpec((1, tt, c), lambda b, i: (b, i, 0)),
        scratch_shapes=[pltpu.VMEM((CONV_HALO + tt, c), F32), pltpu.VMEM((tt, c), F32)],
        compiler_params=_cparams(2),
        name="conformer_conv",
    )(ag, ag, ag, ag, conv_w, vec(conv_b), vec(ln_g), vec(ln_b), vec(out_norm))


def _attn_kernel(qia_ref, qib_ref, w_ref, ki_ref, q_ref, k_ref, v_ref, on_ref, o_ref,
                 key_scr, bias_scr, y_scr, s_scr, p_scr, m_scr, a_scr, acc_scr, qit_scr, qaug_scr,
                 *, topk, idx_scale, attn_scale, att_strip):
    i = pl.program_id(1)
    n_att = lax.shift_right_logical(i * Q_BLOCK, int(math.log2(ATT_CHUNK))) + 1
    halves = ATT_CHUNK // KEY_CHUNK
    n_key = n_att * halves
    n_heads = q_ref.shape[2] // HEAD_DIM
    group = n_heads // N_KV_HEADS
    i32 = jnp.int32
    q_pos = i * Q_BLOCK + lax.broadcasted_iota(i32, (KEY_CHUNK, Q_BLOCK), 1)
    row_iota = lax.broadcasted_iota(i32, (KEY_CHUNK, Q_BLOCK), 0)
    att_rows = lax.broadcasted_iota(i32, (ATT_CHUNK, Q_BLOCK), 0)

    def fold(x):
        return x.reshape(ATT_CHUNK // SUBLANE, SUBLANE, Q_BLOCK).sum(axis=0)

    pair = 2 * IDX_DIM
    for half, ref in enumerate((qia_ref, qib_ref)):
        for hp in range(ref.shape[2] // pair):
            t = ref[0, :, hp * pair:(hp + 1) * pair].astype(F32).T
            c0 = (half * (ref.shape[2] // pair) + hp) * 2 * Q_BLOCK
            qit_scr[:, c0:c0 + Q_BLOCK] = t[:IDX_DIM].astype(qit_scr.dtype)
            qit_scr[:, c0 + Q_BLOCK:c0 + 2 * Q_BLOCK] = t[IDX_DIM:].astype(qit_scr.dtype)

    def score_body(kc, kmax):
        r0 = pl.multiple_of(kc * KEY_CHUNK, KEY_CHUNK)
        ki_c = ki_ref[0, pl.ds(r0, KEY_CHUNK), :]
        acc = jnp.zeros((KEY_CHUNK, Q_BLOCK), F32)
        for hp in range(IDX_HEADS // 2):
            cols = slice(hp * 2 * Q_BLOCK, (hp + 1) * 2 * Q_BLOCK)
            s = jnp.dot(ki_c, qit_scr[:, cols], preferred_element_type=F32)
            r = jnp.maximum(s, 0.0) * w_ref[0, 0, :, cols]
            acc = acc + (r[:, :Q_BLOCK] + r[:, Q_BLOCK:])
        sc = acc * idx_scale
        sc = jnp.where(sc == 0.0, 0.0, sc)
        bits = lax.bitcast_convert_type(sc, i32)
        key = jnp.where(bits < 0, bits ^ i32(INT_MAX), bits)
        key = jnp.where(r0 + row_iota <= q_pos, key, i32(INT_MIN))
        key_scr[pl.ds(r0, KEY_CHUNK), :] = key
        return jnp.maximum(kmax, key.reshape(KEY_CHUNK // SUBLANE, SUBLANE, Q_BLOCK).max(axis=0))

    kmax = lax.fori_loop(0, n_key, score_body, jnp.full((SUBLANE, Q_BLOCK), INT_MIN, i32))
    kmax = kmax.max(axis=0, keepdims=True)

    def count(pred):
        def body(c, acc):
            r0 = pl.multiple_of(c * ATT_CHUNK, ATT_CHUNK)
            return acc + fold(jnp.where(pred(key_scr[pl.ds(r0, ATT_CHUNK), :], r0), 1, 0))
        return lax.fori_loop(0, n_att, body, jnp.zeros((SUBLANE, Q_BLOCK), i32)).sum(axis=0, keepdims=True)

    def bis_cond(st):
        return jnp.logical_and(st[3] > 0, st[4] < MAX_BISECT)

    def bis_body(st):
        lo, hi, cnt_lo, _, it = st
        for _ in range(BISECT_UNROLL):
            mid = (lo >> 1) + (hi >> 1) + (lo & hi & 1)
            cnt = count(lambda k, r0, mid=mid: k >= mid)
            ge = cnt >= topk
            hi = jnp.where(cnt == topk, mid + 1, jnp.where(ge, hi, mid))
            lo = jnp.where(ge, mid, lo)
            cnt_lo = jnp.where(ge, cnt, cnt_lo)
        active = jnp.max(jnp.where(hi - 1 > lo, 1, 0))
        return lo, hi, cnt_lo, active, it + BISECT_UNROLL

    lo0 = jnp.full((1, Q_BLOCK), INT_MIN, i32)
    hi0 = jnp.where(kmax == INT_MAX, kmax, kmax + 1)
    cnt0 = jnp.full((1, Q_BLOCK), 1, i32) * (n_att * ATT_CHUNK)
    thr, _, cnt_thr, _, _ = lax.while_loop(bis_cond, bis_body, (lo0, hi0, cnt0, i32(1), i32(0)))

    excess = jnp.where(thr > INT_MIN, cnt_thr - topk, 0)

    @pl.when(jnp.max(excess) > 0)
    def _():
        need = topk - count(lambda k, r0: k > thr)

        def pos_body(_, st):
            lo_p, hi_p = st
            mid = (lo_p + hi_p) >> 1
            ok = count(lambda k, r0: jnp.logical_and(k == thr, r0 + att_rows <= mid)) >= need
            return jnp.where(ok, lo_p, mid), jnp.where(ok, mid, hi_p)

        n_rows = n_att * ATT_CHUNK
        _, cut = lax.fori_loop(0, int(math.log2(key_scr.shape[0])) + 1, pos_body,
                               (jnp.full((1, Q_BLOCK), -1, i32), jnp.full((1, Q_BLOCK), 1, i32) * (n_rows - 1)))
        cut = jnp.where(excess > 0, cut, INT_MAX)

        def demote_body(c, carry):
            r0 = pl.multiple_of(c * ATT_CHUNK, ATT_CHUNK)
            k = key_scr[pl.ds(r0, ATT_CHUNK), :]
            drop = jnp.logical_and(k == thr, r0 + att_rows > cut)
            key_scr[pl.ds(r0, ATT_CHUNK), :] = jnp.where(drop, k - 1, k)
            return carry

        lax.fori_loop(0, n_att, demote_body, 0)

    thr = jnp.maximum(thr, i32(INT_MIN + 1))

    def bias_body(kc, carry):
        r0 = pl.multiple_of(kc * KEY_CHUNK, KEY_CHUNK)
        b_t = jnp.where(key_scr[pl.ds(r0, KEY_CHUNK), :] >= thr, 0.0, NEG_BIAS)
        bias_scr[pl.ds(r0, KEY_CHUNK), :] = b_t.astype(bias_scr.dtype)
        return carry

    lax.fori_loop(0, n_key, bias_body, 0)

    c_exp = attn_scale * math.log2(math.e)
    rows = group * Q_BLOCK
    m_scr[...] = jnp.full(m_scr.shape, M_INIT, F32)
    acc_scr[...] = jnp.zeros(acc_scr.shape, F32)
    onehot = ((lax.broadcasted_iota(i32, (rows, Q_BLOCK), 0) & (Q_BLOCK - 1))
              == lax.broadcasted_iota(i32, (rows, Q_BLOCK), 1))
    for g in range(N_KV_HEADS):
        for j in range(group):
            qaug_scr[g, j * Q_BLOCK:(j + 1) * Q_BLOCK, :HEAD_DIM] = (
                q_ref[0, :, (g * group + j) * HEAD_DIM:(g * group + j + 1) * HEAD_DIM])
        qaug_scr[g, :, HEAD_DIM:] = jnp.where(onehot, 1.0, 0.0).astype(qaug_scr.dtype)

    def att_body(c, carry):
        r0 = pl.multiple_of(c * ATT_CHUNK, ATT_CHUNK)
        bias_t = bias_scr[pl.ds(r0, ATT_CHUNK), :]
        for g in range(N_KV_HEADS):
            kk = jnp.concatenate([k_ref[0, pl.ds(r0, ATT_CHUNK), g * HEAD_DIM:(g + 1) * HEAD_DIM], bias_t], axis=1)
            s_scr[g] = lax.dot_general(qaug_scr[g], kk, (((1,), (1,)), ((), ())), preferred_element_type=F32)
        for g in range(N_KV_HEADS):
            for t0 in range(0, rows, att_strip):
                strip = slice(t0, t0 + att_strip)
                x = s_scr[g, strip, :]
                m_old = m_scr[g, strip, :]
                m_new = jnp.maximum(m_old, x.max(axis=1, keepdims=True))
                a_scr[g, strip, :] = jnp.exp2((m_old - m_new) * c_exp)
                m_scr[g, strip, :] = m_new
                p = jnp.exp2((x - jnp.tile(m_new, (1, ATT_CHUNK // LANE))) * c_exp)
                p_scr[g, strip, :] = p.astype(p_scr.dtype)
        for g in range(N_KV_HEADS):
            vv = v_ref[0, pl.ds(r0, ATT_CHUNK), g * HEAD_DIM:(g + 1) * HEAD_DIM]
            v_ones = jnp.concatenate([vv, jnp.ones_like(vv)], axis=1)
            pv = jnp.dot(p_scr[g], v_ones, preferred_element_type=F32)
            acc_scr[g] = jnp.tile(a_scr[g], (1, 2)) * acc_scr[g] + pv
        return carry

    lax.fori_loop(0, n_att, att_body, 0)
    for g in range(N_KV_HEADS):
        acc = acc_scr[g]
        o = acc[:, :HEAD_DIM] / acc[:, HEAD_DIM:]
        for j in range(group):
            y_scr[:, (g * group + j) * HEAD_DIM:(g * group + j + 1) * HEAD_DIM] = o[j * Q_BLOCK:(j + 1) * Q_BLOCK]

    y = y_scr[...]
    o_ref[0] = (y * lax.rsqrt(jnp.mean(y * y, axis=-1, keepdims=True) + EPS) * on_ref[0]).astype(o_ref.dtype)


def _dsa_attention(qkvqi, kw, out_norm, layer, attn_width, *, att_strip=ATT_STRIP):
    nb, seq, _ = qkvqi.shape
    kv_width = N_KV_HEADS * HEAD_DIM
    nblk = seq // Q_BLOCK
    topk = min(TOPK_MAX, seq // 4)
    idx_scale = (IDX_DIM ** -0.5) * (IDX_HEADS ** -0.5)
    attn_scale = HEAD_DIM ** -0.5
    qi_half = IDX_HEADS * IDX_DIM // 2
    qi_blk = (attn_width + 2 * kv_width) // qi_half
    ki = kw[:, :, :IDX_DIM].astype(BF16)
    wi = kw[:, :, IDX_DIM:IDX_DIM + IDX_HEADS]
    wrow = wi.reshape(nb, nblk, Q_BLOCK, IDX_HEADS).transpose(0, 1, 3, 2).reshape(nb, nblk, 1, IDX_HEADS * Q_BLOCK)
    v_blk = (attn_width + kv_width) // kv_width
    rows = attn_width // kv_width * Q_BLOCK
    return pl.pallas_call(
        functools.partial(_attn_kernel, topk=topk, idx_scale=idx_scale, attn_scale=attn_scale,
                          att_strip=att_strip),
        out_shape=jax.ShapeDtypeStruct((nb, seq, attn_width), BF16),
        grid=(nb, nblk),
        in_specs=[
            pl.BlockSpec((1, Q_BLOCK, qi_half), lambda b, i: (b, i, qi_blk)),
            pl.BlockSpec((1, Q_BLOCK, qi_half), lambda b, i: (b, i, qi_blk + 1)),
            pl.BlockSpec((1, 1, 1, IDX_HEADS * Q_BLOCK), lambda b, i: (b, i, 0, 0)),
            pl.BlockSpec((1, seq, IDX_DIM), lambda b, i: (b, 0, 0)),
            pl.BlockSpec((1, Q_BLOCK, attn_width), lambda b, i: (b, i, 0)),
            pl.BlockSpec((1, seq, kv_width), lambda b, i: (b, 0, v_blk - 1)),
            pl.BlockSpec((1, seq, kv_width), lambda b, i: (b, 0, v_blk)),
            pl.BlockSpec((1, 1, attn_width), lambda b, i: (layer, 0, 0)),
        ],
        out_specs=pl.BlockSpec((1, Q_BLOCK, attn_width), lambda b, i: (b, i, 0)),
        scratch_shapes=[
            pltpu.VMEM((seq, Q_BLOCK), jnp.int32),
            pltpu.VMEM((seq, Q_BLOCK), BF16),
            pltpu.VMEM((Q_BLOCK, attn_width), F32),
            pltpu.VMEM((N_KV_HEADS, rows, ATT_CHUNK), F32),
            pltpu.VMEM((N_KV_HEADS, rows, ATT_CHUNK), BF16),
            pltpu.VMEM((N_KV_HEADS, rows, LANE), F32),
            pltpu.VMEM((N_KV_HEADS, rows, LANE), F32),
            pltpu.VMEM((N_KV_HEADS, rows, 2 * HEAD_DIM), F32),
            pltpu.VMEM((IDX_DIM, IDX_HEADS * Q_BLOCK), BF16),
            pltpu.VMEM((N_KV_HEADS, rows, HEAD_DIM + Q_BLOCK), BF16),
        ],
        compiler_params=_cparams(2),
        name="dsa_attention",
    )(qkvqi, qkvqi, wrow, ki, qkvqi, qkvqi, qkvqi, out_norm.reshape(-1, 1, attn_width))


def _ffn(x2d, seq, layer, norm_g, shift, scale, gate, wgu, wd):
    h = _norm_mod(x2d, norm_g, layer, scale, shift, seq)
    act = _matmul_swiglu(h, wgu, layer)
    return _matmul_residual(act, wd, layer, x2d, gate, seq, coef=FFN_RES)


def kernel(x, c, ada_w, ada_b, ffn1_norm, ffn1_wgu, ffn1_wd, mix_norm, w_in, conv_w, conv_b, conv_ln_g,
           conv_ln_b, conv_out_norm, attn_out_norm, w_out, ffn2_norm, ffn2_wgu, ffn2_wd, final_norm):
    nb, seq, d = x.shape
    depth = ada_w.shape[0]
    conv_width = d // 2
    attn_width = d - conv_width
    kv_width = N_KV_HEADS * HEAD_DIM
    n_main = 2 * conv_width + attn_width + 2 * kv_width + IDX_HEADS * IDX_DIM
    n_tail = IDX_DIM + IDX_HEADS
    w_in_t = jnp.transpose(w_in, (0, 2, 1))
    w_tail_t = jnp.pad(w_in_t[:, n_main:n_main + n_tail, :], ((0, 0), (0, LANE - n_tail), (0, 0)))
    wd1, wd2 = _to_bf16(ffn1_wd), _to_bf16(ffn2_wd)

    mod = _modulation(c, ada_w, ada_b)
    x2d = x.reshape(nb * seq, d)
    for l in range(depth):
        sh1, sc1, g1, sh2, sc2, g2, sh3, sc3, g3 = (mod[l, :, i * d:(i + 1) * d] for i in range(N_MOD))
        x2d = _ffn(x2d, seq, l, ffn1_norm, sh1, sc1, g1, ffn1_wgu, wd1)

        h = _norm_mod(x2d, mix_norm, l, sc2, sh2, seq)
        ag = _matmul_nt(h, w_in_t, l, col0=0, ncols=2 * conv_width, out_dtype=F32, bm=1024, bn=512)
        qkvqi = _matmul_nt(h, w_in_t, l, col0=2 * conv_width, ncols=n_main - 2 * conv_width, out_dtype=BF16,
                           bm=1024, bn=512)
        kw = _matmul_nt(h, w_tail_t, l, col0=0, ncols=LANE, out_dtype=F32, bm=1024, bn=LANE)
        y_conv = _conformer_conv(ag.reshape(nb, seq, 2 * conv_width), l, conv_w, conv_b, conv_ln_g,
                                 conv_ln_b, conv_out_norm)
        y_attn = _dsa_attention(qkvqi.reshape(nb, seq, -1), kw.reshape(nb, seq, LANE), attn_out_norm, l,
                                attn_width)
        x2d = _matmul_residual2(y_conv.reshape(nb * seq, conv_width), y_attn.reshape(nb * seq, attn_width),
                                w_out, l, x2d, g2, seq, coef=1.0)

        x2d = _ffn(x2d, seq, l, ffn2_norm, sh3, sc3, g3, ffn2_wgu, wd2)
    return _final_norm(x2d, final_norm).reshape(nb, seq, d)
```
